```python
import jax, jax.numpy as jnp
from jax import lax
import numpy as np

D_MODEL = 1024
BATCH = 4
SEQ = 4096
DEPTH = 4

CHUNK = 64
GMLP_BLOCK = 128
A_HEADS = 4
A_HEAD_DIM = 128
D_A = A_HEADS * A_HEAD_DIM
B_GROUPS = 8
D_B = 512
B_CONV = 3
C_WINDOWS = (2, 4, 8, 16)
C_GROUPS = len(C_WINDOWS)
D_C = D_MODEL // C_GROUPS
D_FF = 2816
FFN_CONV = 3
N_EVEN = (DEPTH + 1) // 2
N_ODD = DEPTH // 2
D_IN_EVEN = 2 * D_A + 3 * D_B
ALPHA = (2.0 * DEPTH) ** 0.25
BETA = (8.0 * DEPTH) ** -0.25
LN_EPS = 1e-5

kernel_name = "hybrid_gmlp_shortconv_pool_convffn_deepnorm"


def layer_norm(x, g, b):
    xf = x.astype(jnp.float32)
    mu = jnp.mean(xf, axis=-1, keepdims=True)
    var = jnp.mean(jnp.square(xf - mu), axis=-1, keepdims=True)
    y = (xf - mu) * lax.rsqrt(var + LN_EPS)
    return (y * g.astype(jnp.float32) + b.astype(jnp.float32)).astype(x.dtype)


def causal_dwconv(h, w):
    K = w.shape[0]
    S = h.shape[1]
    hp = jnp.pad(h, ((0, 0), (K - 1, 0), (0, 0)))
    y = w[0] * hp[:, 0:S]
    for k in range(1, K):
        y = y + w[k] * hp[:, k:k + S]
    return y


def gmlp_chunk_mask():
    c = jnp.arange(GMLP_BLOCK) // CHUNK
    return c[None, :] <= c[:, None]


def gmlp_mixer(uv, ws, bs, ln_g, ln_b):
    Bn, S, _ = uv.shape
    u, v = jnp.split(jax.nn.gelu(uv), 2, axis=-1)
    v = v.reshape(Bn, S, A_HEADS, A_HEAD_DIM)
    v = layer_norm(v, ln_g.reshape(A_HEADS, A_HEAD_DIM), ln_b.reshape(A_HEADS, A_HEAD_DIM))
    v = v.reshape(Bn, S // GMLP_BLOCK, GMLP_BLOCK, A_HEADS, A_HEAD_DIM)
    w = jnp.where(gmlp_chunk_mask()[None], ws, jnp.zeros((), ws.dtype))
    s = jnp.einsum('hij,bnjhd->bnihd', w, v) + bs.T[None, None, :, :, None]
    return u * s.reshape(Bn, S, D_A)


def shortconv_mixer(bch, conv_w):
    gb, gc, h = jnp.split(bch, 3, axis=-1)
    return gb * causal_dwconv(gc * h, conv_w)


def pool_mixer(x, w_c, scale):
    Bn, S, _ = x.shape
    xf = x.astype(jnp.float32)
    cs = jnp.cumsum(xf, axis=1)
    t = jnp.arange(1, S + 1, dtype=jnp.float32)[None, :, None]
    outs = []
    for gi, win in enumerate(C_WINDOWS):
        c = cs[..., gi * D_C:(gi + 1) * D_C]
        prev = jnp.pad(c[:, :-win], ((0, 0), (win, 0), (0, 0)))
        mean = (c - prev) / jnp.minimum(t, jnp.float32(win))
        outs.append(mean - xf[..., gi * D_C:(gi + 1) * D_C])
    p = jnp.stack(outs, axis=2).astype(x.dtype)
    y = jnp.einsum('bsgc,gcd->bsgd', p, w_c).reshape(Bn, S, D_MODEL)
    return y * scale


def conv_ffn(x, w_up, b_up, conv_w, conv_b, w_down):
    h = x @ w_up + b_up
    h = causal_dwconv(h, conv_w) + conv_b
    g, v = jnp.split(h, 2, axis=-1)
    return (jax.nn.gelu(g) * v) @ w_down


def setup_inputs(seed: int = 0) -> dict:
    key = jax.random.key(seed)
    ks = jax.random.split(key, 20)
    f32 = jnp.float32
    nrm = lambda k, shape, s: jax.random.normal(k, shape, f32) * s
    return {
        "x": nrm(ks[0], (BATCH, SEQ, D_MODEL), 1.0),
        "w_in_even": nrm(ks[1], (N_EVEN, D_MODEL, D_IN_EVEN), D_MODEL ** -0.5),
        "gmlp_ws": nrm(ks[2], (N_EVEN, A_HEADS, GMLP_BLOCK, GMLP_BLOCK), 0.5 * GMLP_BLOCK ** -0.5),
        "gmlp_bs": 1.0 + nrm(ks[3], (N_EVEN, A_HEADS, GMLP_BLOCK), 0.02),
        "gmlp_ln_g": 1.0 + nrm(ks[4], (N_EVEN, D_A), 0.02),
        "gmlp_ln_b": nrm(ks[5], (N_EVEN, D_A), 0.02),
        "sconv_w": nrm(ks[6], (N_EVEN, B_CONV, D_B), B_CONV ** -0.5),
        "w_out_even": nrm(ks[7], (N_EVEN, D_A + D_B, D_MODEL), BETA * (D_A + D_B) ** -0.5),
        "pool_w": nrm(ks[8], (N_ODD, C_GROUPS, D_C, D_C), BETA * D_C ** -0.5),
        "pool_scale": 1.0 + nrm(ks[9], (N_ODD, D_MODEL), 0.02),
        "ffn_w_up": nrm(ks[10], (DEPTH, D_MODEL, 2 * D_FF), D_MODEL ** -0.5),
        "ffn_b_up": nrm(ks[11], (DEPTH, 2 * D_FF), 0.01),
        "ffn_conv_w": nrm(ks[12], (DEPTH, FFN_CONV, 2 * D_FF), FFN_CONV ** -0.5),
        "ffn_conv_b": nrm(ks[13], (DEPTH, 2 * D_FF), 0.01),
        "ffn_w_down": nrm(ks[14], (DEPTH, D_FF, D_MODEL), BETA * D_FF ** -0.5),
        "ln_mix_g": 1.0 + nrm(ks[15], (DEPTH, D_MODEL), 0.02),
        "ln_mix_b": nrm(ks[16], (DEPTH, D_MODEL), 0.02),
        "ln_ffn_g": 1.0 + nrm(ks[17], (DEPTH, D_MODEL), 0.02),
        "ln_ffn_b": nrm(ks[18], (DEPTH, D_MODEL), 0.02),
    }


def reference(x, w_in_even, gmlp_ws, gmlp_bs, gmlp_ln_g, gmlp_ln_b, sconv_w,
              w_out_even, pool_w, pool_scale, ffn_w_up, ffn_b_up, ffn_conv_w,
              ffn_conv_b, ffn_w_down, ln_mix_g, ln_mix_b, ln_ffn_g, ln_ffn_b):
    for layer in range(DEPTH):
        i = layer // 2
        if layer % 2 == 0:
            proj = x @ w_in_even[i]
            ya = gmlp_mixer(proj[..., :2 * D_A], gmlp_ws[i], gmlp_bs[i],
                            gmlp_ln_g[i], gmlp_ln_b[i])
            yb = shortconv_mixer(proj[..., 2 * D_A:], sconv_w[i])
            y = jnp.concatenate([ya, yb], axis=-1) @ w_out_even[i]
        else:
            y = pool_mixer(x, pool_w[i], pool_scale[i])
        x = layer_norm(ALPHA * x + y, ln_mix_g[layer], ln_mix_b[layer])
        f = conv_ffn(x, ffn_w_up[layer], ffn_b_up[layer], ffn_conv_w[layer],
                     ffn_conv_b[layer], ffn_w_down[layer])
        x = layer_norm(ALPHA * x + f, ln_ffn_g[layer], ln_ffn_b[layer])
    return x
```

```python
from functools import partial

import jax
import jax.numpy as jnp
from jax import lax
from jax.experimental import pallas as pl
from jax.experimental.pallas import tpu as pltpu

D_MODEL = 1024
DEPTH = 4
CHUNK = 64
GMLP_BLOCK = 128
A_HEADS = 4
A_HEAD_DIM = 128
D_A = A_HEADS * A_HEAD_DIM
D_B = 512
B_CONV = 3
C_WINDOWS = (2, 4, 8, 16)
C_GROUPS = len(C_WINDOWS)
D_C = D_MODEL // C_GROUPS
D_FF = 2816
FFN_CONV = 3
ALPHA = (2.0 * DEPTH) ** 0.25
LN_EPS = 1e-5

SUBLANES = 8
SEQ_TILE = 512
FF_CHUNK = 256
N_FF_CHUNKS = D_FF // FF_CHUNK
POOL_HIST = 16
VMEM_LIMIT_BYTES = 56 * 1024 * 1024

BF16 = jnp.bfloat16
F32 = jnp.float32


def _layer_norm(z, g, b):
    mu = jnp.mean(z, axis=-1, keepdims=True)
    zc = z - mu
    var = jnp.mean(zc * zc, axis=-1, keepdims=True)
    return zc * lax.rsqrt(var + LN_EPS) * g + b


def _dot(a, b):
    return jnp.dot(a, b, preferred_element_type=F32)


def _causal_conv3(buf_ref, hist_ref, h, w):
    t = h.shape[0]
    buf_ref[0:SUBLANES, :] = hist_ref[...]
    buf_ref[SUBLANES:SUBLANES + t, :] = h
    hist_ref[...] = h[t - SUBLANES:, :]
    h1 = buf_ref[SUBLANES - 1:SUBLANES - 1 + t, :]
    h2 = buf_ref[SUBLANES - 2:SUBLANES - 2 + t, :]
    return w[0:1, :] * h2 + w[1:2, :] * h1 + w[2:3, :] * h


def _mixer_even_kernel(x_ref, win_ref, ws_ref, bsb_ref, lng_ref, lnb_ref, cw_ref, wout_ref,
                       g_ref, b_ref, o_ref, buf_ref, hist_ref):
    @pl.when(pl.program_id(1) == 0)
    def _():
        hist_ref[...] = jnp.zeros_like(hist_ref)

    x = x_ref[...]
    t = x.shape[0]
    xb = x.astype(BF16)

    uv = jax.nn.gelu(_dot(xb, win_ref[:, :2 * D_A]))
    row = lax.broadcasted_iota(jnp.int32, (GMLP_BLOCK, GMLP_BLOCK), 0) // CHUNK
    col = lax.broadcasted_iota(jnp.int32, (GMLP_BLOCK, GMLP_BLOCK), 1) // CHUNK
    mask = col <= row
    ya = []
    for h in range(A_HEADS):
        lanes = slice(h * A_HEAD_DIM, (h + 1) * A_HEAD_DIM)
        u = uv[:, lanes]
        v = uv[:, D_A + h * A_HEAD_DIM:D_A + (h + 1) * A_HEAD_DIM]
        vn = _layer_norm(v, lng_ref[:, lanes], lnb_ref[:, lanes]).astype(BF16)
        wm = jnp.where(mask, ws_ref[h], 0.0).astype(BF16)
        s = [_dot(wm, vn[n * GMLP_BLOCK:(n + 1) * GMLP_BLOCK, :]) + bsb_ref[h]
             for n in range(t // GMLP_BLOCK)]
        ya.append(u * jnp.concatenate(s, axis=0))

    bch = _dot(xb, win_ref[:, 2 * D_A:])
    gb = bch[:, :D_B]
    ch = bch[:, D_B:2 * D_B] * bch[:, 2 * D_B:]
    yb = gb * _causal_conv3(buf_ref, hist_ref, ch, cw_ref[...])

    ycat = jnp.concatenate(ya + [yb], axis=-1).astype(BF16)
    y = _dot(ycat, wout_ref[...])
    o_ref[...] = _layer_norm(ALPHA * x + y, g_ref[...], b_ref[...])


def _mixer_odd_kernel(x_ref, pw_ref, scale_ref, g_ref, b_ref, o_ref, buf_ref, hist_ref):
    j = pl.program_id(1)

    @pl.when(j == 0)
    def _():
        hist_ref[...] = jnp.zeros_like(hist_ref)

    x = x_ref[...]
    t = x.shape[0]
    buf_ref[0:POOL_HIST, :] = hist_ref[...]
    buf_ref[POOL_HIST:POOL_HIST + t, :] = x
    hist_ref[...] = x[t - POOL_HIST:, :]

    pos = j * t + lax.broadcasted_iota(jnp.int32, (t, 1), 0) + 1
    ys = []
    for gi, win in enumerate(C_WINDOWS):
        lanes = slice(gi * D_C, (gi + 1) * D_C)
        xg = x[:, lanes]
        tot = xg
        for k in range(1, win):
            tot = tot + buf_ref[POOL_HIST - k:POOL_HIST - k + t, lanes]
        cnt = jnp.minimum(pos, win).astype(F32)
        p = tot / cnt - xg
        ys.append(_dot(p.astype(BF16), pw_ref[gi]))
    y = jnp.concatenate(ys, axis=-1) * scale_ref[...]
    o_ref[...] = _layer_norm(ALPHA * x + y, g_ref[...], b_ref[...])


def _ffn_kernel(x_ref, wup_ref, bup_ref, cw_ref, cb_ref, wdn_ref, g_ref, b_ref, o_ref,
                buf_ref, hist_ref):
    @pl.when(pl.program_id(1) == 0)
    def _():
        hist_ref[...] = jnp.zeros_like(hist_ref)

    x = x_ref[...]
    xb = x.astype(BF16)
    acc = jnp.zeros(x.shape, F32)
    for c in range(N_FF_CHUNKS):
        h = _dot(xb, wup_ref[c]) + bup_ref[c]
        h = _causal_conv3(buf_ref, hist_ref.at[c], h, cw_ref[c]) + cb_ref[c]
        a = jax.nn.gelu(h[:, :FF_CHUNK]) * h[:, FF_CHUNK:]
        acc = acc + _dot(a.astype(BF16), wdn_ref[c])
    o_ref[...] = _layer_norm(ALPHA * x + acc, g_ref[...], b_ref[...])


def _const_spec(shape, single_buffer=False):
    zeros = (0,) * len(shape)
    if single_buffer:
        return pl.BlockSpec(shape, lambda b, j: zeros, pipeline_mode=pl.Buffered(1))
    return pl.BlockSpec(shape, lambda b, j: zeros)


def _sublayer_call(kernel_fn, name, x, params, scratch_shapes):
    batch, seq, d = x.shape
    x_spec = pl.BlockSpec((None, SEQ_TILE, d), lambda b, j: (b, j, 0))
    return pl.pallas_call(
        kernel_fn,
        name=name,
        grid=(batch, seq // SEQ_TILE),
        in_specs=[x_spec] + [_const_spec(p.shape, sb) for p, sb in params],
        out_specs=x_spec,
        out_shape=jax.ShapeDtypeStruct(x.shape, x.dtype),
        scratch_shapes=scratch_shapes,
        compiler_params=pltpu.CompilerParams(
            dimension_semantics=("arbitrary", "arbitrary"),
            vmem_limit_bytes=VMEM_LIMIT_BYTES),
    )(x, *[p for p, _ in params])


def _row(v):
    return v.reshape(1, -1)


def _chunk_gate_value(a):
    r = a.shape[0]
    a = a.reshape(r, 2, N_FF_CHUNKS, FF_CHUNK)
    return a.transpose(2, 0, 1, 3).reshape(N_FF_CHUNKS, r, 2 * FF_CHUNK)


def _mixer_even(x, w_in, ws, bs, ln_g, ln_b, conv_w, w_out, g, b):
    params = [
        (w_in.astype(BF16), True),
        (ws, False),
        (jnp.broadcast_to(bs[:, :, None], (A_HEADS, GMLP_BLOCK, A_HEAD_DIM)), False),
        (_row(ln_g), False), (_row(ln_b), False),
        (conv_w, False),
        (w_out.astype(BF16), True),
        (_row(g), False), (_row(b), False),
    ]
    scratch = [pltpu.VMEM((SUBLANES + SEQ_TILE, D_B), F32), pltpu.VMEM((SUBLANES, D_B), F32)]
    return _sublayer_call(_mixer_even_kernel, "mixer_even", x, params, scratch)


def _mixer_odd(x, pool_w, scale, g, b):
    params = [(pool_w.astype(BF16), False), (_row(scale), False), (_row(g), False), (_row(b), False)]
    scratch = [pltpu.VMEM((POOL_HIST + SEQ_TILE, D_MODEL), F32), pltpu.VMEM((POOL_HIST, D_MODEL), F32)]
    return _sublayer_call(_mixer_odd_kernel, "mixer_odd", x, params, scratch)


def _ffn(x, w_up, b_up, conv_w, conv_b, w_down, g, b):
    params = [
        (_chunk_gate_value(w_up.astype(BF16)), True),
        (_chunk_gate_value(_row(b_up)), False),
        (_chunk_gate_value(conv_w), False),
        (_chunk_gate_value(_row(conv_b)), False),
        (w_down.astype(BF16).reshape(N_FF_CHUNKS, FF_CHUNK, D_MODEL), True),
        (_row(g), False), (_row(b), False),
    ]
    scratch = [pltpu.VMEM((SUBLANES + SEQ_TILE, 2 * FF_CHUNK), F32),
               pltpu.VMEM((N_FF_CHUNKS, SUBLANES, 2 * FF_CHUNK), F32)]
    return _sublayer_call(_ffn_kernel, "conv_ffn", x, params, scratch)


def kernel(x, w_in_even, gmlp_ws, gmlp_bs, gmlp_ln_g, gmlp_ln_b, sconv_w, w_out_even, pool_w, pool_scale, ffn_w_up, ffn_b_up, ffn_conv_w, ffn_conv_b, ffn_w_down, ln_mix_g, ln_mix_b, ln_ffn_g, ln_ffn_b):
    assert x.shape[1] % SEQ_TILE == 0 and SEQ_TILE % GMLP_BLOCK == 0
    for layer in range(DEPTH):
        i = layer // 2
        if layer % 2 == 0:
            x = _mixer_even(x, w_in_even[i], gmlp_ws[i], gmlp_bs[i], gmlp_ln_g[i], gmlp_ln_b[i],
                            sconv_w[i], w_out_even[i], ln_mix_g[layer], ln_mix_b[layer])
        else:
            x = _mixer_odd(x, pool_w[i], pool_scale[i], ln_mix_g[layer], ln_mix_b[layer])
        x = _ffn(x, ffn_w_up[layer], ffn_b_up[layer], ffn_conv_w[layer], ffn_conv_b[layer],
                 ffn_w_down[layer], ln_ffn_g[layer], ln_ffn_b[layer])
    return x
```

```python
from functools import partial

import jax
import jax.numpy as jnp
from jax import lax
from jax.experimental import pallas as pl
from jax.experimental.pallas import tpu as pltpu

D_MODEL = 1024
DEPTH = 4
CHUNK = 64
GMLP_BLOCK = 128
A_HEADS = 4
A_HEAD_DIM = 128
D_A = A_HEADS * A_HEAD_DIM
D_B = 512
B_CONV = 3
C_WINDOWS = (2, 4, 8, 16)
C_GROUPS = len(C_WINDOWS)
D_C = D_MODEL // C_GROUPS
D_FF = 2816
FFN_CONV = 3
ALPHA = (2.0 * DEPTH) ** 0.25
LN_EPS = 1e-5

LANES = 128
SUBLANES = 8
SEQ_TILE = 512
FF_CHUNK = 256
N_FF_CHUNKS = D_FF // FF_CHUNK
POOL_HIST = 16
VMEM_LIMIT_BYTES = 56 * 1024 * 1024

BF16 = jnp.bfloat16
F32 = jnp.float32


def _layer_norm(z, g, b):
    mu = jnp.mean(z, axis=-1, keepdims=True)
    zc = z - mu
    var = jnp.mean(zc * zc, axis=-1, keepdims=True)
    return zc * lax.rsqrt(var + LN_EPS) * g + b


def _dot(a, b):
    return jnp.dot(a, b, preferred_element_type=F32)


def _causal_conv3(buf_ref, hist_ref, h, w):
    t, c = h.shape
    out = []
    for s in range(c // LANES):
        lanes = slice(s * LANES, (s + 1) * LANES)
        hs = h[:, lanes]
        buf_ref[s, 0:SUBLANES, :] = hist_ref[s]
        buf_ref[s, SUBLANES:SUBLANES + t, :] = hs
        hist_ref[s] = hs[t - SUBLANES:, :]
        h1 = buf_ref[s, SUBLANES - 1:SUBLANES - 1 + t, :]
        h2 = buf_ref[s, SUBLANES - 2:SUBLANES - 2 + t, :]
        out.append(w[0:1, lanes] * h2 + w[1:2, lanes] * h1 + w[2:3, lanes] * hs)
    return jnp.concatenate(out, axis=-1)


def _mixer_even_kernel(x_ref, win_ref, ws_ref, bsb_ref, lng_ref, lnb_ref, cw_ref, wout_ref,
                       g_ref, b_ref, o_ref, buf_ref, hist_ref):
    @pl.when(pl.program_id(1) == 0)
    def _():
        hist_ref[...] = jnp.zeros_like(hist_ref)

    x = x_ref[...]
    t = x.shape[0]
    xb = x.astype(BF16)

    uv = jax.nn.gelu(_dot(xb, win_ref[:, :2 * D_A]))
    row = lax.broadcasted_iota(jnp.int32, (GMLP_BLOCK, GMLP_BLOCK), 0) // CHUNK
    col = lax.broadcasted_iota(jnp.int32, (GMLP_BLOCK, GMLP_BLOCK), 1) // CHUNK
    mask = col <= row
    ya = []
    for h in range(A_HEADS):
        lanes = slice(h * A_HEAD_DIM, (h + 1) * A_HEAD_DIM)
        u = uv[:, lanes]
        v = uv[:, D_A + h * A_HEAD_DIM:D_A + (h + 1) * A_HEAD_DIM]
        vn = _layer_norm(v, lng_ref[:, lanes], lnb_ref[:, lanes]).astype(BF16)
        wm = jnp.where(mask, ws_ref[h], 0.0).astype(BF16)
        s = [_dot(wm, vn[n * GMLP_BLOCK:(n + 1) * GMLP_BLOCK, :]) + bsb_ref[h]
             for n in range(t // GMLP_BLOCK)]
        ya.append(u * jnp.concatenate(s, axis=0))

    bch = _dot(xb, win_ref[:, 2 * D_A:])
    gb = bch[:, :D_B]
    ch = bch[:, D_B:2 * D_B] * bch[:, 2 * D_B:]
    yb = gb * _causal_conv3(buf_ref, hist_ref, ch, cw_ref[...])

    ycat = jnp.concatenate(ya + [yb], axis=-1).astype(BF16)
    y = _dot(ycat, wout_ref[...])
    o_ref[...] = _layer_norm(ALPHA * x + y, g_ref[...], b_ref[...])


def _mixer_odd_kernel(x_ref, pw_ref, scale_ref, g_ref, b_ref, o_ref, buf_ref, hist_ref):
    j = pl.program_id(1)

    @pl.when(j == 0)
    def _():
        hist_ref[...] = jnp.zeros_like(hist_ref)

    x = x_ref[...]
    t = x.shape[0]
    for s in range(D_MODEL // LANES):
        lanes = slice(s * LANES, (s + 1) * LANES)
        buf_ref[s, 0:POOL_HIST, :] = hist_ref[s]
        buf_ref[s, POOL_HIST:POOL_HIST + t, :] = x[:, lanes]
        hist_ref[s] = x[t - POOL_HIST:, lanes]

    pos = j * t + lax.broadcasted_iota(jnp.int32, (t, 1), 0) + 1
    ys = []
    for gi, win in enumerate(C_WINDOWS):
        tot = []
        for s in range(gi * D_C // LANES, (gi + 1) * D_C // LANES):
            acc = x[:, s * LANES:(s + 1) * LANES]
            for k in range(1, win):
                acc = acc + buf_ref[s, POOL_HIST - k:POOL_HIST - k + t, :]
            tot.append(acc)
        xg = x[:, gi * D_C:(gi + 1) * D_C]
        cnt = jnp.minimum(pos, win).astype(F32)
        p = jnp.concatenate(tot, axis=-1) / cnt - xg
        ys.append(_dot(p.astype(BF16), pw_ref[gi]))
    y = jnp.concatenate(ys, axis=-1) * scale_ref[...]
    o_ref[...] = _layer_norm(ALPHA * x + y, g_ref[...], b_ref[...])


def _gelu_tanh(x):
    k = (2.0 / jnp.pi) ** 0.5
    return x * (0.5 + 0.5 * jnp.tanh(x * (k + (0.044715 * k) * (x * x))))


def _ffn_kernel(x_ref, wup_ref, bup_ref, cw_ref, cb_ref, wdn_ref, g_ref, b_ref, o_ref,
                buf_ref, hist_ref, act_ref):
    t = x_ref.shape[0]
    slabs = FF_CHUNK // LANES

    @pl.when(pl.program_id(1) == 0)
    def _():
        for slab in range(2 * D_FF // LANES):
            lanes = slice(slab * LANES, (slab + 1) * LANES)
            hist_ref[slab] = jnp.broadcast_to(-bup_ref[:, lanes], (SUBLANES, LANES))

    x = x_ref[...]
    xb = x.astype(BF16)
    for c in range(N_FF_CHUNKS):
        gate = slice(c * FF_CHUNK, (c + 1) * FF_CHUNK)
        value = slice(D_FF + c * FF_CHUNK, D_FF + (c + 1) * FF_CHUNK)
        ys = []
        for k, cols in enumerate((gate, value)):
            h = _dot(xb, wup_ref[:, cols])
            w = cw_ref[:, cols]
            const = cb_ref[:, cols] + bup_ref[:, cols] * (w[0:1, :] + w[1:2, :] + w[2:3, :])
            out = []
            for s in range(slabs):
                lanes = slice(s * LANES, (s + 1) * LANES)
                slab = cols.start // LANES + s
                stage = buf_ref.at[c % 2, k, s]
                stage[0:SUBLANES, :] = hist_ref[slab]
                stage[SUBLANES:SUBLANES + t, :] = h[:, lanes]
                hist_ref[slab] = h[t - SUBLANES:, lanes]
                h0 = stage[SUBLANES:SUBLANES + t, :]
                h1 = stage[SUBLANES - 1:SUBLANES - 1 + t, :]
                h2 = stage[SUBLANES - 2:SUBLANES - 2 + t, :]
                out.append(w[0:1, lanes] * h2 + w[1:2, lanes] * h1 + w[2:3, lanes] * h0
                           + const[:, lanes])
            ys.append(jnp.concatenate(out, axis=-1))
        act_ref[:, gate] = (_gelu_tanh(ys[0]) * ys[1]).astype(BF16)
    f = _dot(act_ref[...], wdn_ref[...])
    o_ref[...] = _layer_norm(ALPHA * x + f, g_ref[...], b_ref[...])


def _const_spec(shape, single_buffer=False):
    zeros = (0,) * len(shape)
    if single_buffer:
        return pl.BlockSpec(shape, lambda b, j: zeros, pipeline_mode=pl.Buffered(1))
    return pl.BlockSpec(shape, lambda b, j: zeros)


def _sublayer_call(kernel_fn, name, x, params, scratch_shapes):
    batch, seq, d = x.shape
    x_spec = pl.BlockSpec((None, SEQ_TILE, d), lambda b, j: (b, j, 0))
    return pl.pallas_call(
        kernel_fn,
        name=name,
        grid=(batch, seq // SEQ_TILE),
        in_specs=[x_spec] + [_const_spec(p.shape, sb) for p, sb in params],
        out_specs=x_spec,
        out_shape=jax.ShapeDtypeStruct(x.shape, x.dtype),
        scratch_shapes=scratch_shapes,
        compiler_params=pltpu.CompilerParams(
            dimension_semantics=("arbitrary", "arbitrary"),
            vmem_limit_bytes=VMEM_LIMIT_BYTES),
    )(x, *[p for p, _ in params])


def _row(v):
    return v.reshape(1, -1)


def _mixer_even(x, w_in, ws, bs, ln_g, ln_b, conv_w, w_out, g, b):
    params = [
        (w_in.astype(BF16), True),
        (ws, False),
        (jnp.broadcast_to(bs[:, :, None], (A_HEADS, GMLP_BLOCK, A_HEAD_DIM)), False),
        (_row(ln_g), False), (_row(ln_b), False),
        (conv_w, False),
        (w_out.astype(BF16), True),
        (_row(g), False), (_row(b), False),
    ]
    scratch = [pltpu.VMEM((D_B // LANES, SUBLANES + SEQ_TILE, LANES), F32),
               pltpu.VMEM((D_B // LANES, SUBLANES, LANES), F32)]
    return _sublayer_call(_mixer_even_kernel, "mixer_even", x, params, scratch)


def _mixer_odd(x, pool_w, scale, g, b):
    params = [(pool_w.astype(BF16), False), (_row(scale), False), (_row(g), False), (_row(b), False)]
    scratch = [pltpu.VMEM((D_MODEL // LANES, POOL_HIST + SEQ_TILE, LANES), F32),
               pltpu.VMEM((D_MODEL // LANES, POOL_HIST, LANES), F32)]
    return _sublayer_call(_mixer_odd_kernel, "mixer_odd", x, params, scratch)


def _ffn(x, w_up, b_up, conv_w, conv_b, w_down, g, b):
    params = [
        (w_up.astype(BF16), True),
        (_row(b_up), False),
        (conv_w, False),
        (_row(conv_b), False),
        (w_down.astype(BF16), True),
        (_row(g), False), (_row(b), False),
    ]
    slabs = FF_CHUNK // LANES
    scratch = [pltpu.VMEM((2, 2, slabs, SUBLANES + SEQ_TILE, LANES), F32),
               pltpu.VMEM((2 * D_FF // LANES, SUBLANES, LANES), F32),
               pltpu.VMEM((SEQ_TILE, D_FF), BF16)]
    return _sublayer_call(_ffn_kernel, "conv_ffn", x, params, scratch)


def kernel(x, w_in_even, gmlp_ws, gmlp_bs, gmlp_ln_g, gmlp_ln_b, sconv_w, w_out_even, pool_w, pool_scale, ffn_w_up, ffn_b_up, ffn_conv_w, ffn_conv_b, ffn_w_down, ln_mix_g, ln_mix_b, ln_ffn_g, ln_ffn_b):
    assert x.shape[1] % SEQ_TILE == 0 and SEQ_TILE % GMLP_BLOCK == 0
    for layer in range(DEPTH):
        i = layer // 2
        if layer % 2 == 0:
            x = _mixer_even(x, w_in_even[i], gmlp_ws[i], gmlp_bs[i], gmlp_ln_g[i], gmlp_ln_b[i],
                            sconv_w[i], w_out_even[i], ln_mix_g[layer], ln_mix_b[layer])
        else:
            x = _mixer_odd(x, pool_w[i], pool_scale[i], ln_mix_g[layer], ln_mix_b[layer])
        x = _ffn(x, ffn_w_up[layer], ffn_b_up[layer], ffn_conv_w[layer], ffn_conv_b[layer],
                 ffn_w_down[layer], ln_ffn_g[layer], ln_ffn_b[layer])
    return x
```

```python
from functools import partial

import jax
import jax.numpy as jnp
from jax import lax
from jax.experimental import pallas as pl
from jax.experimental.pallas import tpu as pltpu

D_MODEL = 1024
DEPTH = 4
CHUNK = 64
GMLP_BLOCK = 128
A_HEADS = 4
A_HEAD_DIM = 128
D_A = A_HEADS * A_HEAD_DIM
D_B = 512
B_CONV = 3
C_WINDOWS = (2, 4, 8, 16)
C_GROUPS = len(C_WINDOWS)
D_C = D_MODEL // C_GROUPS
D_FF = 2816
FFN_CONV = 3
ALPHA = (2.0 * DEPTH) ** 0.25
LN_EPS = 1e-5

LANES = 128
SUBLANES = 8
SEQ_TILE = 512
FF_CHUNK = 256
N_FF_CHUNKS = D_FF // FF_CHUNK
DOWN_COLS = 256
DOWN_BLOCKS = D_MODEL // DOWN_COLS
DOWN_FIRST_CHUNK = 2
DOWN_EVERY = 2
POOL_HIST = 16
VMEM_LIMIT_BYTES = 56 * 1024 * 1024

BF16 = jnp.bfloat16
F32 = jnp.float32


def _layer_norm(z, g, b):
    mu = jnp.mean(z, axis=-1, keepdims=True)
    zc = z - mu
    var = jnp.mean(zc * zc, axis=-1, keepdims=True)
    return zc * lax.rsqrt(var + LN_EPS) * g + b


def _dot(a, b):
    return jnp.dot(a, b, preferred_element_type=F32)


def _causal_conv3(buf_ref, hist_ref, h, w):
    t, c = h.shape
    out = []
    for s in range(c // LANES):
        lanes = slice(s * LANES, (s + 1) * LANES)
        hs = h[:, lanes]
        buf_ref[s, 0:SUBLANES, :] = hist_ref[s]
        buf_ref[s, SUBLANES:SUBLANES + t, :] = hs
        hist_ref[s] = hs[t - SUBLANES:, :]
        h1 = buf_ref[s, SUBLANES - 1:SUBLANES - 1 + t, :]
        h2 = buf_ref[s, SUBLANES - 2:SUBLANES - 2 + t, :]
        out.append(w[0:1, lanes] * h2 + w[1:2, lanes] * h1 + w[2:3, lanes] * hs)
    return jnp.concatenate(out, axis=-1)


def _mixer_even_kernel(x_ref, win_ref, ws_ref, bsb_ref, lng_ref, lnb_ref, cw_ref, wout_ref,
                       g_ref, b_ref, o_ref, buf_ref, hist_ref):
    @pl.when(pl.program_id(1) == 0)
    def _():
        hist_ref[...] = jnp.zeros_like(hist_ref)

    x = x_ref[...]
    t = x.shape[0]
    xb = x.astype(BF16)

    uv = jax.nn.gelu(_dot(xb, win_ref[:, :2 * D_A]))
    row = lax.broadcasted_iota(jnp.int32, (GMLP_BLOCK, GMLP_BLOCK), 0) // CHUNK
    col = lax.broadcasted_iota(jnp.int32, (GMLP_BLOCK, GMLP_BLOCK), 1) // CHUNK
    mask = col <= row
    ya = []
    for h in range(A_HEADS):
        lanes = slice(h * A_HEAD_DIM, (h + 1) * A_HEAD_DIM)
        u = uv[:, lanes]
        v = uv[:, D_A + h * A_HEAD_DIM:D_A + (h + 1) * A_HEAD_DIM]
        vn = _layer_norm(v, lng_ref[:, lanes], lnb_ref[:, lanes]).astype(BF16)
        wm = jnp.where(mask, ws_ref[h], 0.0).astype(BF16)
        s = [_dot(wm, vn[n * GMLP_BLOCK:(n + 1) * GMLP_BLOCK, :]) + bsb_ref[h]
             for n in range(t // GMLP_BLOCK)]
        ya.append(u * jnp.concatenate(s, axis=0))

    bch = _dot(xb, win_ref[:, 2 * D_A:])
    gb = bch[:, :D_B]
    ch = bch[:, D_B:2 * D_B] * bch[:, 2 * D_B:]
    yb = gb * _causal_conv3(buf_ref, hist_ref, ch, cw_ref[...])

    ycat = jnp.concatenate(ya + [yb], axis=-1).astype(BF16)
    y = _dot(ycat, wout_ref[...])
    o_ref[...] = _layer_norm(ALPHA * x + y, g_ref[...], b_ref[...])


def _mixer_odd_kernel(x_ref, pw_ref, scale_ref, g_ref, b_ref, o_ref, buf_ref, hist_ref):
    j = pl.program_id(1)

    @pl.when(j == 0)
    def _():
        hist_ref[...] = jnp.zeros_like(hist_ref)

    x = x_ref[...]
    t = x.shape[0]
    for s in range(D_MODEL // LANES):
        lanes = slice(s * LANES, (s + 1) * LANES)
        buf_ref[s, 0:POOL_HIST, :] = hist_ref[s]
        buf_ref[s, POOL_HIST:POOL_HIST + t, :] = x[:, lanes]
        hist_ref[s] = x[t - POOL_HIST:, lanes]

    pos = j * t + lax.broadcasted_iota(jnp.int32, (t, 1), 0) + 1
    ys = []
    for gi, win in enumerate(C_WINDOWS):
        tot = []
        for s in range(gi * D_C // LANES, (gi + 1) * D_C // LANES):
            acc = x[:, s * LANES:(s + 1) * LANES]
            for k in range(1, win):
                acc = acc + buf_ref[s, POOL_HIST - k:POOL_HIST - k + t, :]
            tot.append(acc)
        xg = x[:, gi * D_C:(gi + 1) * D_C]
        cnt = jnp.minimum(pos, win).astype(F32)
        p = jnp.concatenate(tot, axis=-1) / cnt - xg
        ys.append(_dot(p.astype(BF16), pw_ref[gi]))
    y = jnp.concatenate(ys, axis=-1) * scale_ref[...]
    o_ref[...] = _layer_norm(ALPHA * x + y, g_ref[...], b_ref[...])


def _gelu_tanh(x):
    k = (2.0 / jnp.pi) ** 0.5
    return x * (0.5 + 0.5 * jnp.tanh(x * (k + (0.044715 * k) * (x * x))))


def _ffn_kernel(tiles_per_seq, x_ref, wup_ref, bup_ref, cw_ref, cb_ref, wdn_ref, g_ref, b_ref,
                o_ref, buf_ref, hist_ref, act_ref, xprev_ref):
    t = x_ref.shape[0]
    slabs = FF_CHUNK // LANES
    i = pl.program_id(0)
    cur = i % 2

    @pl.when(i == 0)
    def _():
        act_ref[...] = jnp.zeros_like(act_ref)
        xprev_ref[...] = jnp.zeros_like(xprev_ref)

    @pl.when(i % tiles_per_seq == 0)
    def _():
        for slab in range(2 * D_FF // LANES):
            lanes = slice(slab * LANES, (slab + 1) * LANES)
            hist_ref[slab] = jnp.broadcast_to(-bup_ref[:, lanes], (SUBLANES, LANES))

    x = x_ref[...]
    xb = x.astype(BF16)

    def chunk_cols(c):
        return (slice(c * FF_CHUNK, (c + 1) * FF_CHUNK),
                slice(D_FF + c * FF_CHUNK, D_FF + (c + 1) * FF_CHUNK))

    def up_project(c, k):
        cols = chunk_cols(c)[k]
        h = _dot(xb, wup_ref[:, cols])
        for s in range(slabs):
            lanes = slice(s * LANES, (s + 1) * LANES)
            slab = cols.start // LANES + s
            stage = buf_ref.at[c % 2, k, s]
            stage[0:SUBLANES, :] = hist_ref[slab]
            stage[SUBLANES:SUBLANES + t, :] = h[:, lanes]
            hist_ref[slab] = h[t - SUBLANES:, lanes]

    def conv_and_gate(c, s):
        ys = []
        for k, cols in enumerate(chunk_cols(c)):
            lanes = slice(cols.start + s * LANES, cols.start + (s + 1) * LANES)
            w = cw_ref[:, lanes]
            const = cb_ref[:, lanes] + bup_ref[:, lanes] * (w[0:1, :] + w[1:2, :] + w[2:3, :])
            stage = buf_ref.at[c % 2, k, s]
            h0 = stage[SUBLANES:SUBLANES + t, :]
            h1 = stage[SUBLANES - 1:SUBLANES - 1 + t, :]
            h2 = stage[SUBLANES - 2:SUBLANES - 2 + t, :]
            ys.append(w[0:1, :] * h2 + w[1:2, :] * h1 + w[2:3, :] * h0 + const)
        lanes = slice(c * FF_CHUNK + s * LANES, c * FF_CHUNK + (s + 1) * LANES)
        act_ref[cur, :, lanes] = (_gelu_tanh(ys[0]) * ys[1]).astype(BF16)

    def down_project_block():
        cols = slice(len(fs) * DOWN_COLS, (len(fs) + 1) * DOWN_COLS)
        fs.append(_dot(act_ref[1 - cur], wdn_ref[:, cols]))
        if len(fs) == DOWN_BLOCKS:
            f = jnp.concatenate(fs, axis=-1)
            o_ref[...] = _layer_norm(ALPHA * xprev_ref[...] + f, g_ref[...], b_ref[...])

    fs = []
    up_project(0, 0)
    up_project(0, 1)
    for c in range(N_FF_CHUNKS):
        more = c + 1 < N_FF_CHUNKS
        if more:
            up_project(c + 1, 0)
        conv_and_gate(c, 0)
        if more:
            up_project(c + 1, 1)
        if c >= DOWN_FIRST_CHUNK and (c - DOWN_FIRST_CHUNK) % DOWN_EVERY == 0 and len(fs) < DOWN_BLOCKS:
            down_project_block()
        conv_and_gate(c, 1)
    xprev_ref[...] = x


def _layer_spec(stack, layer, single_buffer):
    shape = (None,) + stack.shape[1:]
    index = (layer,) + (0,) * (stack.ndim - 1)
    if single_buffer:
        return pl.BlockSpec(shape, lambda *_: index, pipeline_mode=pl.Buffered(1))
    return pl.BlockSpec(shape, lambda *_: index)


def _rows(v):
    return v.reshape(v.shape[0], 1, v.shape[1])


def _sublayer_call(kernel_fn, name, x, params, scratch_shapes):
    batch, seq, d = x.shape
    x_spec = pl.BlockSpec((None, SEQ_TILE, d), lambda b, j: (b, j, 0))
    return pl.pallas_call(
        kernel_fn,
        name=name,
        grid=(batch, seq // SEQ_TILE),
        in_specs=[x_spec] + [_layer_spec(*p) for p in params],
        out_specs=x_spec,
        out_shape=jax.ShapeDtypeStruct(x.shape, x.dtype),
        scratch_shapes=scratch_shapes,
        compiler_params=pltpu.CompilerParams(
            dimension_semantics=("arbitrary", "arbitrary"),
            vmem_limit_bytes=VMEM_LIMIT_BYTES),
    )(x, *[p[0] for p in params])


def _mixer_even(x, i, layer, w):
    params = [
        (w["w_in"], i, True),
        (w["gmlp_ws"], i, False),
        (w["gmlp_bs"], i, False),
        (w["gmlp_ln_g"], i, False), (w["gmlp_ln_b"], i, False),
        (w["sconv_w"], i, False),
        (w["w_out"], i, True),
        (w["ln_mix_g"], layer, False), (w["ln_mix_b"], layer, False),
    ]
    scratch = [pltpu.VMEM((D_B // LANES, SUBLANES + SEQ_TILE, LANES), F32),
               pltpu.VMEM((D_B // LANES, SUBLANES, LANES), F32)]
    return _sublayer_call(_mixer_even_kernel, "mixer_even", x, params, scratch)


def _mixer_odd(x, i, layer, w):
    params = [(w["pool_w"], i, False), (w["pool_scale"], i, False),
              (w["ln_mix_g"], layer, False), (w["ln_mix_b"], layer, False)]
    scratch = [pltpu.VMEM((D_MODEL // LANES, POOL_HIST + SEQ_TILE, LANES), F32),
               pltpu.VMEM((D_MODEL // LANES, POOL_HIST, LANES), F32)]
    return _sublayer_call(_mixer_odd_kernel, "mixer_odd", x, params, scratch)


def _ffn(x, layer, w):
    params = [
        (w["ffn_w_up"], layer, True),
        (w["ffn_b_up"], layer, False),
        (w["ffn_conv_w"], layer, False),
        (w["ffn_conv_b"], layer, False),
        (w["ffn_w_down"], layer, True),
        (w["ln_ffn_g"], layer, False), (w["ln_ffn_b"], layer, False),
    ]
    slabs = FF_CHUNK // LANES
    scratch = [pltpu.VMEM((2, 2, slabs, SUBLANES + SEQ_TILE, LANES), F32),
               pltpu.VMEM((2 * D_FF // LANES, SUBLANES, LANES), F32),
               pltpu.VMEM((2, SEQ_TILE, D_FF), BF16),
               pltpu.VMEM((SEQ_TILE, D_MODEL), F32)]
    batch, seq, d = x.shape
    n_tiles = batch * seq // SEQ_TILE
    out = pl.pallas_call(
        partial(_ffn_kernel, seq // SEQ_TILE),
        name="conv_ffn",
        grid=(n_tiles + 1,),
        in_specs=[pl.BlockSpec((SEQ_TILE, d), lambda i: (jnp.minimum(i, n_tiles - 1), 0))]
        + [_layer_spec(*p) for p in params],
        out_specs=pl.BlockSpec((SEQ_TILE, d), lambda i: (jnp.maximum(i - 1, 0), 0)),
        out_shape=jax.ShapeDtypeStruct((batch * seq, d), x.dtype),
        scratch_shapes=scratch,
        compiler_params=pltpu.CompilerParams(
            dimension_semantics=("arbitrary",),
            vmem_limit_bytes=VMEM_LIMIT_BYTES),
    )(x.reshape(batch * seq, d), *[p[0] for p in params])
    return out.reshape(batch, seq, d)


def kernel(x, w_in_even, gmlp_ws, gmlp_bs, gmlp_ln_g, gmlp_ln_b, sconv_w, w_out_even, pool_w, pool_scale, ffn_w_up, ffn_b_up, ffn_conv_w, ffn_conv_b, ffn_w_down, ln_mix_g, ln_mix_b, ln_ffn_g, ln_ffn_b):
    assert x.shape[1] % SEQ_TILE == 0 and SEQ_TILE % GMLP_BLOCK == 0
    w = dict(
        w_in=w_in_even.astype(BF16), w_out=w_out_even.astype(BF16),
        gmlp_ws=gmlp_ws,
        gmlp_bs=jnp.broadcast_to(gmlp_bs[..., None], gmlp_bs.shape + (A_HEAD_DIM,)),
        gmlp_ln_g=_rows(gmlp_ln_g), gmlp_ln_b=_rows(gmlp_ln_b), sconv_w=sconv_w,
        pool_w=pool_w.astype(BF16), pool_scale=_rows(pool_scale),
        ffn_w_up=ffn_w_up.astype(BF16), ffn_b_up=_rows(ffn_b_up), ffn_conv_w=ffn_conv_w,
        ffn_conv_b=_rows(ffn_conv_b), ffn_w_down=ffn_w_down.astype(BF16),
        ln_mix_g=_rows(ln_mix_g), ln_mix_b=_rows(ln_mix_b),
        ln_ffn_g=_rows(ln_ffn_g), ln_ffn_b=_rows(ln_ffn_b),
    )
    for layer in range(DEPTH):
        if layer % 2 == 0:
            x = _mixer_even(x, layer // 2, layer, w)
        else:
            x = _mixer_odd(x, layer // 2, layer, w)
        x = _ffn(x, layer, w)
    return x
```

```python
from functools import partial

import jax
import jax.numpy as jnp
from jax import lax
from jax.experimental import pallas as pl
from jax.experimental.pallas import tpu as pltpu

D_MODEL = 1024
DEPTH = 4
CHUNK = 64
GMLP_BLOCK = 128
A_HEADS = 4
A_HEAD_DIM = 128
D_A = A_HEADS * A_HEAD_DIM
D_B = 512
B_CONV = 3
C_WINDOWS = (2, 4, 8, 16)
C_GROUPS = len(C_WINDOWS)
D_C = D_MODEL // C_GROUPS
D_FF = 2816
FFN_CONV = 3
ALPHA = (2.0 * DEPTH) ** 0.25
LN_EPS = 1e-5

LANES = 128
SUBLANES = 8
BF16_ROWS = 16
SEQ_TILE = 512
FF_CHUNK = 256
N_FF_CHUNKS = D_FF // FF_CHUNK
DOWN_COLS = 256
DOWN_BLOCKS = D_MODEL // DOWN_COLS
DOWN_FIRST_CHUNK = 2
DOWN_EVERY = 2
POOL_PLAN = (0, 1, 2, 3, 3, 4, 5, 6, 6, 7)
CONVERT_BLOCKS = 32
POOL_HIST = 16
VMEM_LIMIT_BYTES = 56 * 1024 * 1024

BF16 = jnp.bfloat16
F32 = jnp.float32


def _layer_norm(z, g, b):
    mu = jnp.mean(z, axis=-1, keepdims=True)
    zc = z - mu
    var = jnp.mean(zc * zc, axis=-1, keepdims=True)
    return zc * lax.rsqrt(var + LN_EPS) * g + b


def _dot(a, b):
    return jnp.dot(a, b, preferred_element_type=F32)


def _convert_blocks(src_refs, dst_refs):
    for src, dst in zip(src_refs, dst_refs, strict=True):
        dst[...] = src[...].astype(BF16)


def _causal_conv3(buf_ref, hist_ref, h, w):
    t, c = h.shape
    out = []
    for s in range(c // LANES):
        lanes = slice(s * LANES, (s + 1) * LANES)
        hs = h[:, lanes]
        buf_ref[s, 0:SUBLANES, :] = hist_ref[s]
        buf_ref[s, SUBLANES:SUBLANES + t, :] = hs
        hist_ref[s] = hs[t - SUBLANES:, :]
        h1 = buf_ref[s, SUBLANES - 1:SUBLANES - 1 + t, :]
        h2 = buf_ref[s, SUBLANES - 2:SUBLANES - 2 + t, :]
        out.append(w[0:1, lanes] * h2 + w[1:2, lanes] * h1 + w[2:3, lanes] * hs)
    return jnp.concatenate(out, axis=-1)


def _mixer_even_kernel(n_convert, x_ref, win_ref, ws_ref, bsb_ref, lng_ref, lnb_ref, cw_ref,
                       wout_ref, g_ref, b_ref, *refs):
    o_ref, buf_ref, hist_ref = refs[n_convert], refs[-2], refs[-1]
    _convert_blocks(refs[:n_convert], refs[n_convert + 1:-2])

    @pl.when(pl.program_id(1) == 0)
    def _():
        hist_ref[...] = jnp.zeros_like(hist_ref)

    x = x_ref[...]
    t = x.shape[0]
    xb = x.astype(BF16)

    uv = jax.nn.gelu(_dot(xb, win_ref[:, :2 * D_A]))
    row = lax.broadcasted_iota(jnp.int32, (GMLP_BLOCK, GMLP_BLOCK), 0) // CHUNK
    col = lax.broadcasted_iota(jnp.int32, (GMLP_BLOCK, GMLP_BLOCK), 1) // CHUNK
    mask = col <= row
    ya = []
    for h in range(A_HEADS):
        lanes = slice(h * A_HEAD_DIM, (h + 1) * A_HEAD_DIM)
        u = uv[:, lanes]
        v = uv[:, D_A + h * A_HEAD_DIM:D_A + (h + 1) * A_HEAD_DIM]
        vn = _layer_norm(v, lng_ref[:, lanes], lnb_ref[:, lanes]).astype(BF16)
        wm = jnp.where(mask, ws_ref[h], 0.0).astype(BF16)
        s = [_dot(wm, vn[n * GMLP_BLOCK:(n + 1) * GMLP_BLOCK, :]) + bsb_ref[h]
             for n in range(t // GMLP_BLOCK)]
        ya.append(u * jnp.concatenate(s, axis=0))

    bch = _dot(xb, win_ref[:, 2 * D_A:])
    gb = bch[:, :D_B]
    ch = bch[:, D_B:2 * D_B] * bch[:, 2 * D_B:]
    yb = gb * _causal_conv3(buf_ref, hist_ref, ch, cw_ref[...])

    ycat = jnp.concatenate(ya + [yb], axis=-1).astype(BF16)
    y = _dot(ycat, wout_ref[...])
    o_ref[...] = _layer_norm(ALPHA * x + y, g_ref[...], b_ref[...])


def _pool_mixer_steps(j, pw_ref, scale_ref, g_ref, b_ref, buf_ref, hist_ref, o_ref):
    t = o_ref.shape[0]
    rows = slice(POOL_HIST, POOL_HIST + t)

    def stage(x):
        for s in range(D_MODEL // LANES):
            lanes = slice(s * LANES, (s + 1) * LANES)
            buf_ref[s, 0:POOL_HIST, :] = hist_ref[s]
            buf_ref[s, rows, :] = x[:, lanes]
            hist_ref[s] = x[t - POOL_HIST:, lanes]

    def group(gi):
        win = C_WINDOWS[gi]
        pos = j * t + lax.broadcasted_iota(jnp.int32, (t, 1), 0) + 1
        cnt = jnp.minimum(pos, win).astype(F32)
        p = []
        for s in range(gi * D_C // LANES, (gi + 1) * D_C // LANES):
            xs = buf_ref[s, rows, :]
            acc = xs
            for k in range(1, win):
                acc = acc + buf_ref[s, POOL_HIST - k:POOL_HIST - k + t, :]
            p.append(acc / cnt - xs)
        lanes = slice(gi * D_C, (gi + 1) * D_C)
        y = _dot(jnp.concatenate(p, axis=-1).astype(BF16), pw_ref[gi])
        o_ref[:, lanes] = y * scale_ref[:, lanes]

    def finish():
        x = jnp.concatenate([buf_ref[s, rows, :] for s in range(D_MODEL // LANES)], axis=-1)
        o_ref[...] = _layer_norm(ALPHA * x + o_ref[...], g_ref[...], b_ref[...])

    return stage, group, finish


def _gelu_tanh(x):
    k = (2.0 / jnp.pi) ** 0.5
    return x * (0.5 + 0.5 * jnp.tanh(x * (k + (0.044715 * k) * (x * x))))


def _ffn_kernel(tiles_per_seq, with_pool, n_convert, x_ref, wup_ref, bup_ref, cw_ref, cb_ref,
                wdn_ref, g_ref, b_ref, *refs):
    if with_pool:
        pw_ref, pscale_ref, pg_ref, pb_ref = refs[:4]
        refs = refs[4:]
    _convert_blocks(refs[:n_convert], refs[n_convert + 1:2 * n_convert + 1])
    o_ref = refs[n_convert]
    if with_pool:
        buf_ref, hist_ref, act_ref, xprev_ref, pbuf_ref, phist_ref = refs[2 * n_convert + 1:]
    else:
        buf_ref, hist_ref, act_ref, xprev_ref = refs[2 * n_convert + 1:]
    t = x_ref.shape[0]
    slabs = FF_CHUNK // LANES
    i = pl.program_id(0)
    cur = i % 2
    prev_tile = (i + tiles_per_seq - 1) % tiles_per_seq

    @pl.when(i == 0)
    def _():
        act_ref[...] = jnp.zeros_like(act_ref)
        xprev_ref[...] = jnp.zeros_like(xprev_ref)

    if with_pool:
        @pl.when(jnp.logical_or(i == 0, prev_tile == 0))
        def _():
            phist_ref[...] = jnp.zeros_like(phist_ref)

    @pl.when(i % tiles_per_seq == 0)
    def _():
        for slab in range(2 * D_FF // LANES):
            lanes = slice(slab * LANES, (slab + 1) * LANES)
            hist_ref[slab] = jnp.broadcast_to(-bup_ref[:, lanes], (SUBLANES, LANES))

    def step(project_up):
        x = x_ref[...]
        xb = x.astype(BF16)

        def chunk_cols(c):
            return (slice(c * FF_CHUNK, (c + 1) * FF_CHUNK),
                    slice(D_FF + c * FF_CHUNK, D_FF + (c + 1) * FF_CHUNK))

        def up_project(c, k):
            cols = chunk_cols(c)[k]
            h = _dot(xb, wup_ref[:, cols])
            for s in range(slabs):
                lanes = slice(s * LANES, (s + 1) * LANES)
                slab = cols.start // LANES + s
                stage = buf_ref.at[c % 2, k, s]
                stage[0:SUBLANES, :] = hist_ref[slab]
                stage[SUBLANES:SUBLANES + t, :] = h[:, lanes]
                hist_ref[slab] = h[t - SUBLANES:, lanes]

        def conv_and_gate(c, s):
            ys = []
            for k, cols in enumerate(chunk_cols(c)):
                lanes = slice(cols.start + s * LANES, cols.start + (s + 1) * LANES)
                w = cw_ref[:, lanes]
                const = cb_ref[:, lanes] + bup_ref[:, lanes] * (w[0:1, :] + w[1:2, :] + w[2:3, :])
                stage = buf_ref.at[c % 2, k, s]
                h0 = stage[SUBLANES:SUBLANES + t, :]
                h1 = stage[SUBLANES - 1:SUBLANES - 1 + t, :]
                h2 = stage[SUBLANES - 2:SUBLANES - 2 + t, :]
                ys.append(w[0:1, :] * h2 + w[1:2, :] * h1 + w[2:3, :] * h0 + const)
            lanes = slice(c * FF_CHUNK + s * LANES, c * FF_CHUNK + (s + 1) * LANES)
            act_ref[cur, :, lanes] = (_gelu_tanh(ys[0]) * ys[1]).astype(BF16)

        def down_project_block():
            cols = slice(len(fs) * DOWN_COLS, (len(fs) + 1) * DOWN_COLS)
            fs.append(_dot(act_ref[1 - cur], wdn_ref[:, cols]))

        def finish_previous_tile():
            f = jnp.concatenate(fs, axis=-1)
            y = _layer_norm(ALPHA * xprev_ref[...] + f, g_ref[...], b_ref[...])
            if with_pool:
                pool_stage(y)
            else:
                o_ref[...] = y

        fs = []
        plan = {}
        if with_pool:
            pool_stage, pool_group, pool_finish = _pool_mixer_steps(
                prev_tile, pw_ref, pscale_ref, pg_ref, pb_ref, pbuf_ref, phist_ref, o_ref)
            pieces = [down_project_block] * DOWN_BLOCKS + [finish_previous_tile]
            pieces += [partial(pool_group, gi) for gi in reversed(range(C_GROUPS))] + [pool_finish]
            for c, piece in zip(POOL_PLAN, pieces, strict=True):
                plan.setdefault(c, []).append(piece)
        else:
            for n in range(DOWN_BLOCKS):
                plan[DOWN_FIRST_CHUNK + n * DOWN_EVERY] = [down_project_block]
            plan[DOWN_FIRST_CHUNK + (DOWN_BLOCKS - 1) * DOWN_EVERY].append(finish_previous_tile)
        if not project_up:
            for c in sorted(plan):
                for piece in plan[c]:
                    piece()
            return
        up_project(0, 0)
        up_project(0, 1)
        for c in range(N_FF_CHUNKS):
            more = c + 1 < N_FF_CHUNKS
            if more:
                up_project(c + 1, 0)
            conv_and_gate(c, 0)
            if more:
                up_project(c + 1, 1)
            for piece in plan.get(c, ()):
                piece()
            conv_and_gate(c, 1)
        xprev_ref[...] = x

    pl.when(i < pl.num_programs(0) - 1)(partial(step, True))
    pl.when(i == pl.num_programs(0) - 1)(partial(step, False))


def _layer_spec(stack, layer, single_buffer):
    shape = (None,) + stack.shape[1:]
    index = (layer,) + (0,) * (stack.ndim - 1)
    if single_buffer:
        return pl.BlockSpec(shape, lambda *_: index, pipeline_mode=pl.Buffered(1))
    return pl.BlockSpec(shape, lambda *_: index)


def _rows(v):
    return v.reshape(v.shape[0], 1, v.shape[1])


def _convert_specs(jobs, step_of):
    in_specs, out_specs, out_shapes, operands = [], [], [], []
    for stack, layer in jobs:
        _, r, c = stack.shape
        rows = r // CONVERT_BLOCKS
        assert rows * CONVERT_BLOCKS == r and rows % BF16_ROWS == 0
        block = lambda *idx: jnp.minimum(step_of(*idx), CONVERT_BLOCKS - 1)
        in_specs.append(pl.BlockSpec((None, rows, c), lambda *idx, l=layer, b=block: (l, b(*idx), 0)))
        out_specs.append(pl.BlockSpec((rows, c), lambda *idx, b=block: (b(*idx), 0)))
        out_shapes.append(jax.ShapeDtypeStruct((r, c), BF16))
        operands.append(stack)
    return in_specs, out_specs, out_shapes, operands


def _mixer_even(x, i, layer, w, convert):
    params = [
        (w["w_in"], 0, True),
        (w["gmlp_ws"], i, False),
        (w["gmlp_bs"], i, False),
        (w["gmlp_ln_g"], i, False), (w["gmlp_ln_b"], i, False),
        (w["sconv_w"], i, False),
        (w["w_out"], 0, True),
        (w["ln_mix_g"], layer, False), (w["ln_mix_b"], layer, False),
    ]
    scratch = [pltpu.VMEM((D_B // LANES, SUBLANES + SEQ_TILE, LANES), F32),
               pltpu.VMEM((D_B // LANES, SUBLANES, LANES), F32)]
    batch, seq, d = x.shape
    tiles_per_seq = seq // SEQ_TILE
    x_spec = pl.BlockSpec((None, SEQ_TILE, d), lambda b, j: (b, j, 0))
    c_in, c_out, c_shapes, c_ops = _convert_specs(convert, lambda b, j: b * tiles_per_seq + j)
    out, *converted = pl.pallas_call(
        partial(_mixer_even_kernel, len(convert)),
        name="mixer_even",
        grid=(batch, tiles_per_seq),
        in_specs=[x_spec] + [_layer_spec(*p) for p in params] + c_in,
        out_specs=[x_spec] + c_out,
        out_shape=[jax.ShapeDtypeStruct(x.shape, x.dtype)] + c_shapes,
        scratch_shapes=scratch,
        compiler_params=pltpu.CompilerParams(
            dimension_semantics=("arbitrary", "arbitrary"),
            vmem_limit_bytes=VMEM_LIMIT_BYTES),
    )(x, *[p[0] for p in params], *c_ops)
    return out, converted


def _ffn(x, layer, w, convert, pool_layer=None):
    params = [
        (w["ffn_w_up"], 0, True),
        (w["ffn_b_up"], layer, False),
        (w["ffn_conv_w"], layer, False),
        (w["ffn_conv_b"], layer, False),
        (w["ffn_w_down"], 0, True),
        (w["ln_ffn_g"], layer, False), (w["ln_ffn_b"], layer, False),
    ]
    slabs = FF_CHUNK // LANES
    scratch = [pltpu.VMEM((2, 2, slabs, SUBLANES + SEQ_TILE, LANES), F32),
               pltpu.VMEM((2 * D_FF // LANES, SUBLANES, LANES), F32),
               pltpu.VMEM((2, SEQ_TILE, D_FF), BF16),
               pltpu.VMEM((SEQ_TILE, D_MODEL), F32)]
    with_pool = pool_layer is not None
    if with_pool:
        params += [(w["pool_w"], pool_layer // 2, False), (w["pool_scale"], pool_layer // 2, False),
                   (w["ln_mix_g"], pool_layer, False), (w["ln_mix_b"], pool_layer, False)]
        scratch += [pltpu.VMEM((D_MODEL // LANES, POOL_HIST + SEQ_TILE, LANES), F32),
                    pltpu.VMEM((D_MODEL // LANES, POOL_HIST, LANES), F32)]
    batch, seq, d = x.shape
    n_tiles = batch * seq // SEQ_TILE
    c_in, c_out, c_shapes, c_ops = _convert_specs(convert, lambda i: i)
    out, *converted = pl.pallas_call(
        partial(_ffn_kernel, seq // SEQ_TILE, with_pool, len(convert)),
        name="conv_ffn",
        grid=(n_tiles + 1,),
        in_specs=[pl.BlockSpec((SEQ_TILE, d), lambda i: (jnp.minimum(i, n_tiles - 1), 0))]
        + [_layer_spec(*p) for p in params] + c_in,
        out_specs=[pl.BlockSpec((SEQ_TILE, d), lambda i: (jnp.maximum(i - 1, 0), 0))] + c_out,
        out_shape=[jax.ShapeDtypeStruct((batch * seq, d), x.dtype)] + c_shapes,
        scratch_shapes=scratch,
        compiler_params=pltpu.CompilerParams(
            dimension_semantics=("arbitrary",),
            vmem_limit_bytes=VMEM_LIMIT_BYTES),
    )(x.reshape(batch * seq, d), *[p[0] for p in params], *c_ops)
    return out.reshape(batch, seq, d), converted


def kernel(x, w_in_even, gmlp_ws, gmlp_bs, gmlp_ln_g, gmlp_ln_b, sconv_w, w_out_even, pool_w, pool_scale, ffn_w_up, ffn_b_up, ffn_conv_w, ffn_conv_b, ffn_w_down, ln_mix_g, ln_mix_b, ln_ffn_g, ln_ffn_b):
    assert x.shape[1] % SEQ_TILE == 0 and SEQ_TILE % GMLP_BLOCK == 0
    assert x.shape[0] * x.shape[1] // SEQ_TILE >= CONVERT_BLOCKS
    w = dict(
        gmlp_ws=gmlp_ws,
        gmlp_bs=jnp.broadcast_to(gmlp_bs[..., None], gmlp_bs.shape + (A_HEAD_DIM,)),
        gmlp_ln_g=_rows(gmlp_ln_g), gmlp_ln_b=_rows(gmlp_ln_b), sconv_w=sconv_w,
        pool_w=pool_w.astype(BF16), pool_scale=_rows(pool_scale),
        ffn_b_up=_rows(ffn_b_up), ffn_conv_w=ffn_conv_w, ffn_conv_b=_rows(ffn_conv_b),
        ln_mix_g=_rows(ln_mix_g), ln_mix_b=_rows(ln_mix_b),
        ln_ffn_g=_rows(ln_ffn_g), ln_ffn_b=_rows(ln_ffn_b),
    )
    w_down_rows = ffn_w_down.reshape(DEPTH, 2 * D_FF, D_MODEL // 2)
    ffn_jobs = lambda layer: [(ffn_w_up, layer), (w_down_rows, layer)]

    def use_ffn(converted):
        w["ffn_w_up"] = converted[0][None]
        w["ffn_w_down"] = converted[1].reshape(D_FF, D_MODEL)[None]

    w["w_in"], w["w_out"] = w_in_even[:1].astype(BF16), w_out_even[:1].astype(BF16)
    for layer in range(0, DEPTH, 2):
        x, converted = _mixer_even(x, layer // 2, layer, w, ffn_jobs(layer))
        use_ffn(converted)
        x, converted = _ffn(x, layer, w, ffn_jobs(layer + 1), pool_layer=layer + 1)
        use_ffn(converted)
        more = layer + 2 < DEPTH
        jobs = [(w_in_even, layer // 2 + 1), (w_out_even, layer // 2 + 1)] if more else []
        x, converted = _ffn(x, layer + 1, w, jobs)
        if more:
            w["w_in"], w["w_out"] = converted[0][None], converted[1][None]
    return x
```

```python
from functools import partial

import jax
import jax.numpy as jnp
from jax import lax
from jax.experimental import pallas as pl
from jax.experimental.pallas import tpu as pltpu

D_MODEL = 1024
DEPTH = 4
CHUNK = 64
GMLP_BLOCK = 128
A_HEADS = 4
A_HEAD_DIM = 128
D_A = A_HEADS * A_HEAD_DIM
D_B = 512
B_CONV = 3
C_WINDOWS = (2, 4, 8, 16)
C_GROUPS = len(C_WINDOWS)
D_C = D_MODEL // C_GROUPS
D_FF = 2816
FFN_CONV = 3
ALPHA = (2.0 * DEPTH) ** 0.25
LN_EPS = 1e-5

LANES = 128
SUBLANES = 8
BF16_ROWS = 16
SEQ_TILE = 512
FF_CHUNK = 256
N_FF_CHUNKS = D_FF // FF_CHUNK
DOWN_COLS = 256
DOWN_BLOCKS = D_MODEL // DOWN_COLS
DOWN_FIRST_CHUNK = 2
DOWN_EVERY = 2
POOL_PLAN = (0, 1, 2, 3, 3, 4, 5, 6, 6, 7)
CONVERT_BLOCKS = 32
POOL_HIST = 16
VMEM_LIMIT_BYTES = 56 * 1024 * 1024

BF16 = jnp.bfloat16
F32 = jnp.float32


def _layer_norm(z, g, b):
    mu = jnp.mean(z, axis=-1, keepdims=True)
    zc = z - mu
    var = jnp.mean(zc * zc, axis=-1, keepdims=True)
    return zc * lax.rsqrt(var + LN_EPS) * g + b


def _dot(a, b):
    return jnp.dot(a, b, preferred_element_type=F32)


def _convert_blocks(src_refs, dst_refs):
    for src, dst in zip(src_refs, dst_refs, strict=True):
        dst[...] = src[...].astype(BF16)


def _causal_conv3(buf_ref, hist_ref, h, w):
    t, c = h.shape
    out = []
    for s in range(c // LANES):
        lanes = slice(s * LANES, (s + 1) * LANES)
        hs = h[:, lanes]
        buf_ref[s, 0:SUBLANES, :] = hist_ref[s]
        buf_ref[s, SUBLANES:SUBLANES + t, :] = hs
        hist_ref[s] = hs[t - SUBLANES:, :]
        h1 = buf_ref[s, SUBLANES - 1:SUBLANES - 1 + t, :]
        h2 = buf_ref[s, SUBLANES - 2:SUBLANES - 2 + t, :]
        out.append(w[0:1, lanes] * h2 + w[1:2, lanes] * h1 + w[2:3, lanes] * hs)
    return jnp.concatenate(out, axis=-1)


def _mixer_even_kernel(n_convert, x_ref, win_ref, ws_ref, bsb_ref, lng_ref, lnb_ref, cw_ref,
                       wout_ref, g_ref, b_ref, *refs):
    o_ref, buf_ref, hist_ref = refs[n_convert], refs[-2], refs[-1]
    _convert_blocks(refs[:n_convert], refs[n_convert + 1:-2])

    @pl.when(pl.program_id(1) == 0)
    def _():
        hist_ref[...] = jnp.zeros_like(hist_ref)

    x = x_ref[...]
    t = x.shape[0]
    xb = x.astype(BF16)

    uv = jax.nn.gelu(_dot(xb, win_ref[:, :2 * D_A]))
    row = lax.broadcasted_iota(jnp.int32, (GMLP_BLOCK, GMLP_BLOCK), 0) // CHUNK
    col = lax.broadcasted_iota(jnp.int32, (GMLP_BLOCK, GMLP_BLOCK), 1) // CHUNK
    mask = col <= row
    ya = []
    for h in range(A_HEADS):
        lanes = slice(h * A_HEAD_DIM, (h + 1) * A_HEAD_DIM)
        u = uv[:, lanes]
        v = uv[:, D_A + h * A_HEAD_DIM:D_A + (h + 1) * A_HEAD_DIM]
        vn = _layer_norm(v, lng_ref[:, lanes], lnb_ref[:, lanes]).astype(BF16)
        wm = jnp.where(mask, ws_ref[h], 0.0).astype(BF16)
        s = [_dot(wm, vn[n * GMLP_BLOCK:(n + 1) * GMLP_BLOCK, :]) + bsb_ref[h]
             for n in range(t // GMLP_BLOCK)]
        ya.append(u * jnp.concatenate(s, axis=0))

    bch = _dot(xb, win_ref[:, 2 * D_A:])
    gb = bch[:, :D_B]
    ch = bch[:, D_B:2 * D_B] * bch[:, 2 * D_B:]
    yb = gb * _causal_conv3(buf_ref, hist_ref, ch, cw_ref[...])

    ycat = jnp.concatenate(ya + [yb], axis=-1).astype(BF16)
    y = _dot(ycat, wout_ref[...])
    o_ref[...] = _layer_norm(ALPHA * x + y, g_ref[...], b_ref[...])


def _pool_mixer_steps(j, pw_ref, scale_ref, g_ref, b_ref, buf_ref, hist_ref, o_ref):
    t = o_ref.shape[0]
    rows = slice(POOL_HIST, POOL_HIST + t)

    def stage(x):
        for s in range(D_MODEL // LANES):
            lanes = slice(s * LANES, (s + 1) * LANES)
            buf_ref[s, 0:POOL_HIST, :] = hist_ref[s]
            buf_ref[s, rows, :] = x[:, lanes]
            hist_ref[s] = x[t - POOL_HIST:, lanes]

    def group(gi):
        win = C_WINDOWS[gi]
        pos = j * t + lax.broadcasted_iota(jnp.int32, (t, 1), 0) + 1
        cnt = jnp.minimum(pos, win).astype(F32)
        p = []
        for s in range(gi * D_C // LANES, (gi + 1) * D_C // LANES):
            xs = buf_ref[s, rows, :]
            acc = xs
            for k in range(1, win):
                acc = acc + buf_ref[s, POOL_HIST - k:POOL_HIST - k + t, :]
            p.append(acc / cnt - xs)
        lanes = slice(gi * D_C, (gi + 1) * D_C)
        y = _dot(jnp.concatenate(p, axis=-1).astype(BF16), pw_ref[gi])
        o_ref[:, lanes] = y * scale_ref[:, lanes]

    def finish():
        x = jnp.concatenate([buf_ref[s, rows, :] for s in range(D_MODEL // LANES)], axis=-1)
        o_ref[...] = _layer_norm(ALPHA * x + o_ref[...], g_ref[...], b_ref[...])

    return stage, group, finish


def _gelu_tanh(x):
    k = (2.0 / jnp.pi) ** 0.5
    return x * (0.5 + 0.5 * jnp.tanh(x * (k + (0.044715 * k) * (x * x))))


def _ffn_kernel(tiles_per_seq, with_pool, n_convert, x_ref, wup_ref, bup_ref, cw_ref, cb_ref,
                wdn_ref, g_ref, b_ref, *refs):
    if with_pool:
        pw_ref, pscale_ref, pg_ref, pb_ref = refs[:4]
        refs = refs[4:]
    _convert_blocks(refs[:n_convert], refs[n_convert + 1:2 * n_convert + 1])
    o_ref = refs[n_convert]
    if with_pool:
        buf_ref, hist_ref, act_ref, xprev_ref, pbuf_ref, phist_ref = refs[2 * n_convert + 1:]
    else:
        buf_ref, hist_ref, act_ref, xprev_ref = refs[2 * n_convert + 1:]
    t = x_ref.shape[0]
    slabs = FF_CHUNK // LANES
    i = pl.program_id(0)
    cur = i % 2
    prev_tile = (i + tiles_per_seq - 1) % tiles_per_seq

    @pl.when(i == 0)
    def _():
        act_ref[...] = jnp.zeros_like(act_ref)
        xprev_ref[...] = jnp.zeros_like(xprev_ref)

    if with_pool:
        @pl.when(jnp.logical_or(i == 0, prev_tile == 0))
        def _():
            phist_ref[...] = jnp.zeros_like(phist_ref)

    @pl.when(i % tiles_per_seq == 0)
    def _():
        for slab in range(2 * D_FF // LANES):
            lanes = slice(slab * LANES, (slab + 1) * LANES)
            hist_ref[slab] = jnp.broadcast_to(-bup_ref[:, lanes], (SUBLANES, LANES))

    def step(project_up):
        x = x_ref[...]
        xb = x.astype(BF16)

        def chunk_cols(c):
            return (slice(c * FF_CHUNK, (c + 1) * FF_CHUNK),
                    slice(D_FF + c * FF_CHUNK, D_FF + (c + 1) * FF_CHUNK))

        def up_project(c, k):
            cols = chunk_cols(c)[k]
            h = _dot(xb, wup_ref[:, cols])
            for s in range(slabs):
                lanes = slice(s * LANES, (s + 1) * LANES)
                slab = cols.start // LANES + s
                stage = buf_ref.at[c % 2, k, s]
                stage[0:SUBLANES, :] = hist_ref[slab]
                stage[SUBLANES:SUBLANES + t, :] = h[:, lanes]
                hist_ref[slab] = h[t - SUBLANES:, lanes]

        def conv_and_gate(c, s):
            ys = []
            for k, cols in enumerate(chunk_cols(c)):
                lanes = slice(cols.start + s * LANES, cols.start + (s + 1) * LANES)
                w = cw_ref[:, lanes]
                const = cb_ref[:, lanes] + bup_ref[:, lanes] * (w[0:1, :] + w[1:2, :] + w[2:3, :])
                stage = buf_ref.at[c % 2, k, s]
                h0 = stage[SUBLANES:SUBLANES + t, :]
                h1 = stage[SUBLANES - 1:SUBLANES - 1 + t, :]
                h2 = stage[SUBLANES - 2:SUBLANES - 2 + t, :]
                ys.append(w[0:1, :] * h2 + w[1:2, :] * h1 + w[2:3, :] * h0 + const)
            lanes = slice(c * FF_CHUNK + s * LANES, c * FF_CHUNK + (s + 1) * LANES)
            act_ref[cur, :, lanes] = (_gelu_tanh(ys[0]) * ys[1]).astype(BF16)

        def down_project_block():
            cols = slice(len(fs) * DOWN_COLS, (len(fs) + 1) * DOWN_COLS)
            fs.append(_dot(act_ref[1 - cur], wdn_ref[:, cols]))

        def finish_previous_tile():
            f = jnp.concatenate(fs, axis=-1)
            y = _layer_norm(ALPHA * xprev_ref[...] + f, g_ref[...], b_ref[...])
            if with_pool:
                pool_stage(y)
            else:
                o_ref[...] = y

        fs = []
        plan = {}
        if with_pool:
            pool_stage, pool_group, pool_finish = _pool_mixer_steps(
                prev_tile, pw_ref, pscale_ref, pg_ref, pb_ref, pbuf_ref, phist_ref, o_ref)
            pieces = [down_project_block] * DOWN_BLOCKS + [finish_previous_tile]
            pieces += [partial(pool_group, gi) for gi in reversed(range(C_GROUPS))] + [pool_finish]
            for c, piece in zip(POOL_PLAN, pieces, strict=True):
                plan.setdefault(c, []).append(piece)
        else:
            for n in range(DOWN_BLOCKS):
                plan[DOWN_FIRST_CHUNK + n * DOWN_EVERY] = [down_project_block]
            plan[DOWN_FIRST_CHUNK + (DOWN_BLOCKS - 1) * DOWN_EVERY].append(finish_previous_tile)
        if not project_up:
            for c in sorted(plan):
                for piece in plan[c]:
                    piece()
            return
        up_project(0, 0)
        up_project(0, 1)
        for c in range(N_FF_CHUNKS):
            more = c + 1 < N_FF_CHUNKS
            if more:
                up_project(c + 1, 0)
            conv_and_gate(c, 0)
            if more:
                up_project(c + 1, 1)
            for piece in plan.get(c, ()):
                piece()
            conv_and_gate(c, 1)
        xprev_ref[...] = x

    pl.when(i < pl.num_programs(0) - 1)(partial(step, True))
    pl.when(i == pl.num_programs(0) - 1)(partial(step, False))


def _layer_spec(stack, layer, single_buffer):
    shape = (None,) + stack.shape[1:]
    index = (layer,) + (0,) * (stack.ndim - 1)
    if single_buffer:
        return pl.BlockSpec(shape, lambda *_: index, pipeline_mode=pl.Buffered(1))
    return pl.BlockSpec(shape, lambda *_: index)


def _rows(v):
    return v.reshape(v.shape[0], 1, v.shape[1])


def _convert_specs(jobs, step_of):
    in_specs, out_specs, out_shapes, operands = [], [], [], []
    for stack, layer in jobs:
        _, r, c = stack.shape
        n_blocks = max(n for n in range(1, CONVERT_BLOCKS + 1)
                       if r % n == 0 and (r // n) % BF16_ROWS == 0)
        rows = r // n_blocks
        block = lambda *idx, n=n_blocks: jnp.minimum(step_of(*idx), n - 1)
        in_specs.append(pl.BlockSpec((None, rows, c), lambda *idx, l=layer, b=block: (l, b(*idx), 0)))
        out_specs.append(pl.BlockSpec((rows, c), lambda *idx, b=block: (b(*idx), 0)))
        out_shapes.append(jax.ShapeDtypeStruct((r, c), BF16))
        operands.append(stack)
    return in_specs, out_specs, out_shapes, operands


def _mixer_even(x, i, layer, w, convert):
    params = [
        (w["w_in"], 0, True),
        (w["gmlp_ws"], i, False),
        (w["gmlp_bs"], i, False),
        (w["gmlp_ln_g"], i, False), (w["gmlp_ln_b"], i, False),
        (w["sconv_w"], i, False),
        (w["w_out"], 0, True),
        (w["ln_mix_g"], layer, False), (w["ln_mix_b"], layer, False),
    ]
    scratch = [pltpu.VMEM((D_B // LANES, SUBLANES + SEQ_TILE, LANES), F32),
               pltpu.VMEM((D_B // LANES, SUBLANES, LANES), F32)]
    batch, seq, d = x.shape
    tiles_per_seq = seq // SEQ_TILE
    x_spec = pl.BlockSpec((None, SEQ_TILE, d), lambda b, j: (b, j, 0))
    c_in, c_out, c_shapes, c_ops = _convert_specs(convert, lambda b, j: b * tiles_per_seq + j)
    out, *converted = pl.pallas_call(
        partial(_mixer_even_kernel, len(convert)),
        name="mixer_even",
        grid=(batch, tiles_per_seq),
        in_specs=[x_spec] + [_layer_spec(*p) for p in params] + c_in,
        out_specs=[x_spec] + c_out,
        out_shape=[jax.ShapeDtypeStruct(x.shape, x.dtype)] + c_shapes,
        scratch_shapes=scratch,
        compiler_params=pltpu.CompilerParams(
            dimension_semantics=("arbitrary", "arbitrary"),
            vmem_limit_bytes=VMEM_LIMIT_BYTES),
    )(x, *[p[0] for p in params], *c_ops)
    return out, converted


def _ffn(x, layer, w, convert, pool_layer=None):
    params = [
        (w["ffn_w_up"], 0, True),
        (w["ffn_b_up"], layer, False),
        (w["ffn_conv_w"], layer, False),
        (w["ffn_conv_b"], layer, False),
        (w["ffn_w_down"], 0, True),
        (w["ln_ffn_g"], layer, False), (w["ln_ffn_b"], layer, False),
    ]
    slabs = FF_CHUNK // LANES
    scratch = [pltpu.VMEM((2, 2, slabs, SUBLANES + SEQ_TILE, LANES), F32),
               pltpu.VMEM((2 * D_FF // LANES, SUBLANES, LANES), F32),
               pltpu.VMEM((2, SEQ_TILE, D_FF), BF16),
               pltpu.VMEM((SEQ_TILE, D_MODEL), F32)]
    with_pool = pool_layer is not None
    if with_pool:
        params += [(w["pool_w"], pool_layer // 2, False), (w["pool_scale"], pool_layer // 2, False),
                   (w["ln_mix_g"], pool_layer, False), (w["ln_mix_b"], pool_layer, False)]
        scratch += [pltpu.VMEM((D_MODEL // LANES, POOL_HIST + SEQ_TILE, LANES), F32),
                    pltpu.VMEM((D_MODEL // LANES, POOL_HIST, LANES), F32)]
    batch, seq, d = x.shape
    n_tiles = batch * seq // SEQ_TILE
    c_in, c_out, c_shapes, c_ops = _convert_specs(convert, lambda i: i)
    out, *converted = pl.pallas_call(
        partial(_ffn_kernel, seq // SEQ_TILE, with_pool, len(convert)),
        name="conv_ffn",
        grid=(n_tiles + 1,),
        in_specs=[pl.BlockSpec((SEQ_TILE, d), lambda i: (jnp.minimum(i, n_tiles - 1), 0))]
        + [_layer_spec(*p) for p in params] + c_in,
        out_specs=[pl.BlockSpec((SEQ_TILE, d), lambda i: (jnp.maximum(i - 1, 0), 0))] + c_out,
        out_shape=[jax.ShapeDtypeStruct((batch * seq, d), x.dtype)] + c_shapes,
        scratch_shapes=scratch,
        compiler_params=pltpu.CompilerParams(
            dimension_semantics=("arbitrary",),
            vmem_limit_bytes=VMEM_LIMIT_BYTES),
    )(x.reshape(batch * seq, d), *[p[0] for p in params], *c_ops)
    return out.reshape(batch, seq, d), converted


def kernel(x, w_in_even, gmlp_ws, gmlp_bs, gmlp_ln_g, gmlp_ln_b, sconv_w, w_out_even, pool_w, pool_scale, ffn_w_up, ffn_b_up, ffn_conv_w, ffn_conv_b, ffn_w_down, ln_mix_g, ln_mix_b, ln_ffn_g, ln_ffn_b):
    assert x.shape[1] % SEQ_TILE == 0 and SEQ_TILE % GMLP_BLOCK == 0
    assert x.shape[0] * x.shape[1] // SEQ_TILE >= CONVERT_BLOCKS
    w = dict(
        gmlp_ws=gmlp_ws,
        gmlp_bs=jnp.broadcast_to(gmlp_bs[..., None], gmlp_bs.shape + (A_HEAD_DIM,)),
        gmlp_ln_g=_rows(gmlp_ln_g), gmlp_ln_b=_rows(gmlp_ln_b), sconv_w=sconv_w,
        pool_w=pool_w.astype(BF16), pool_scale=_rows(pool_scale),
        ffn_b_up=_rows(ffn_b_up), ffn_conv_w=ffn_conv_w, ffn_conv_b=_rows(ffn_conv_b),
        ln_mix_g=_rows(ln_mix_g), ln_mix_b=_rows(ln_mix_b),
        ln_ffn_g=_rows(ln_ffn_g), ln_ffn_b=_rows(ln_ffn_b),
    )
    ffn_jobs = lambda layer: [(ffn_w_up, layer), (ffn_w_down, layer)]

    def use_ffn(converted):
        w["ffn_w_up"], w["ffn_w_down"] = converted[0][None], converted[1][None]

    w["w_in"], w["w_out"] = w_in_even[:1].astype(BF16), w_out_even[:1].astype(BF16)
    for layer in range(0, DEPTH, 2):
        x, converted = _mixer_even(x, layer // 2, layer, w, ffn_jobs(layer))
        use_ffn(converted)
        x, converted = _ffn(x, layer, w, ffn_jobs(layer + 1), pool_layer=layer + 1)
        use_ffn(converted)
        more = layer + 2 < DEPTH
        jobs = [(w_in_even, layer // 2 + 1), (w_out_even, layer // 2 + 1)] if more else []
        x, converted = _ffn(x, layer + 1, w, jobs)
        if more:
            w["w_in"], w["w_out"] = converted[0][None], converted[1][None]
    return x
```

```python
from functools import partial

import jax
import jax.numpy as jnp
from jax import lax
from jax.experimental import pallas as pl
from jax.experimental.pallas import tpu as pltpu

D_MODEL = 1024
DEPTH = 4
CHUNK = 64
GMLP_BLOCK = 128
A_HEADS = 4
A_HEAD_DIM = 128
D_A = A_HEADS * A_HEAD_DIM
D_B = 512
B_CONV = 3
C_WINDOWS = (2, 4, 8, 16)
C_GROUPS = len(C_WINDOWS)
D_C = D_MODEL // C_GROUPS
D_FF = 2816
FFN_CONV = 3
ALPHA = (2.0 * DEPTH) ** 0.25
LN_EPS = 1e-5

LANES = 128
SUBLANES = 8
BF16_ROWS = 16
SEQ_TILE = 512
FF_CHUNK = 256
N_FF_CHUNKS = D_FF // FF_CHUNK
DOWN_ROWS = 256
DOWN_PLAN = (3, 8)
POOL_PLAN = (1, 4, 6, 7, 8, 8, 9)
CONVERT_BLOCKS = 32
OUT_ROWS = 256
POOL_HIST = 16
VMEM_LIMIT_BYTES = 56 * 1024 * 1024

BF16 = jnp.bfloat16
F32 = jnp.float32


def _layer_norm(z, g, b):
    mu = jnp.mean(z, axis=-1, keepdims=True)
    zc = z - mu
    var = jnp.mean(zc * zc, axis=-1, keepdims=True)
    return zc * lax.rsqrt(var + LN_EPS) * g + b


def _dot(a, b):
    return jnp.dot(a, b, preferred_element_type=F32)


def _convert_blocks(src_refs, dst_refs):
    for src, dst in zip(src_refs, dst_refs, strict=True):
        dst[...] = src[...].astype(BF16)


def _causal_conv3(buf_ref, hist_ref, h, w):
    t, c = h.shape
    out = []
    for s in range(c // LANES):
        lanes = slice(s * LANES, (s + 1) * LANES)
        hs = h[:, lanes]
        buf_ref[s, 0:SUBLANES, :] = hist_ref[s]
        buf_ref[s, SUBLANES:SUBLANES + t, :] = hs
        hist_ref[s] = hs[t - SUBLANES:, :]
        h1 = buf_ref[s, SUBLANES - 1:SUBLANES - 1 + t, :]
        h2 = buf_ref[s, SUBLANES - 2:SUBLANES - 2 + t, :]
        out.append(w[0:1, lanes] * h2 + w[1:2, lanes] * h1 + w[2:3, lanes] * hs)
    return jnp.concatenate(out, axis=-1)


def _mixer_even_kernel(n_convert, x_ref, win_ref, ws_ref, bsb_ref, lng_ref, lnb_ref, cw_ref,
                       wout_ref, g_ref, b_ref, *refs):
    o_ref, buf_ref, hist_ref = refs[n_convert], refs[-2], refs[-1]
    _convert_blocks(refs[:n_convert], refs[n_convert + 1:-2])

    @pl.when(pl.program_id(1) == 0)
    def _():
        hist_ref[...] = jnp.zeros_like(hist_ref)

    x = x_ref[...]
    t = x.shape[0]
    xb = x.astype(BF16)

    uv = _dot(xb, win_ref[:, :2 * D_A])
    bch = _dot(xb, win_ref[:, 2 * D_A:])

    uv = jax.nn.gelu(uv)
    row = lax.broadcasted_iota(jnp.int32, (GMLP_BLOCK, GMLP_BLOCK), 0) // CHUNK
    col = lax.broadcasted_iota(jnp.int32, (GMLP_BLOCK, GMLP_BLOCK), 1) // CHUNK
    mask = col <= row
    ya = []
    for h in range(A_HEADS):
        lanes = slice(h * A_HEAD_DIM, (h + 1) * A_HEAD_DIM)
        u = uv[:, lanes]
        v = uv[:, D_A + h * A_HEAD_DIM:D_A + (h + 1) * A_HEAD_DIM]
        vn = _layer_norm(v, lng_ref[:, lanes], lnb_ref[:, lanes]).astype(BF16)
        wm = jnp.where(mask, ws_ref[h], 0.0).astype(BF16)
        s = [_dot(wm, vn[n * GMLP_BLOCK:(n + 1) * GMLP_BLOCK, :]) + bsb_ref[h]
             for n in range(t // GMLP_BLOCK)]
        ya.append(u * jnp.concatenate(s, axis=0))

    gb = bch[:, :D_B]
    ch = bch[:, D_B:2 * D_B] * bch[:, 2 * D_B:]
    yb = gb * _causal_conv3(buf_ref, hist_ref, ch, cw_ref[...])

    ycat = jnp.concatenate(ya + [yb], axis=-1).astype(BF16)
    for r in range(0, t, OUT_ROWS):
        rows = slice(r, r + OUT_ROWS)
        y = _dot(ycat[rows], wout_ref[...])
        o_ref[rows, :] = _layer_norm(ALPHA * x[rows] + y, g_ref[...], b_ref[...])


def _pool_mixer_steps(j, pw_ref, scale_ref, g_ref, b_ref, buf_ref, hist_ref, o_ref):
    t = o_ref.shape[0]
    rows = slice(POOL_HIST, POOL_HIST + t)

    def stage(r, x):
        for s in range(D_MODEL // LANES):
            lanes = slice(s * LANES, (s + 1) * LANES)
            if r == 0:
                buf_ref[s, 0:POOL_HIST, :] = hist_ref[s]
            buf_ref[s, POOL_HIST + r:POOL_HIST + r + x.shape[0], :] = x[:, lanes]
            if r + x.shape[0] == t:
                hist_ref[s] = x[x.shape[0] - POOL_HIST:, lanes]

    def group(gi):
        win = C_WINDOWS[gi]
        pos = j * t + lax.broadcasted_iota(jnp.int32, (t, 1), 0) + 1
        cnt = jnp.minimum(pos, win).astype(F32)
        p = []
        for s in range(gi * D_C // LANES, (gi + 1) * D_C // LANES):
            xs = buf_ref[s, rows, :]
            acc = xs
            for k in range(1, win):
                acc = acc + buf_ref[s, POOL_HIST - k:POOL_HIST - k + t, :]
            p.append(acc / cnt - xs)
        lanes = slice(gi * D_C, (gi + 1) * D_C)
        y = _dot(jnp.concatenate(p, axis=-1).astype(BF16), pw_ref[gi])
        o_ref[:, lanes] = y * scale_ref[:, lanes]

    def finish():
        x = jnp.concatenate([buf_ref[s, rows, :] for s in range(D_MODEL // LANES)], axis=-1)
        o_ref[...] = _layer_norm(ALPHA * x + o_ref[...], g_ref[...], b_ref[...])

    return stage, group, finish


def _gelu_tanh(x):
    k = (2.0 / jnp.pi) ** 0.5
    return x * (0.5 + 0.5 * jnp.tanh(x * (k + (0.044715 * k) * (x * x))))


def _ffn_kernel(tiles_per_seq, with_pool, n_convert, x_ref, wup_ref, bup_ref, cw_ref, cb_ref,
                wdn_ref, g_ref, b_ref, *refs):
    if with_pool:
        pw_ref, pscale_ref, pg_ref, pb_ref = refs[:4]
        refs = refs[4:]
    _convert_blocks(refs[:n_convert], refs[n_convert + 1:2 * n_convert + 1])
    o_ref = refs[n_convert]
    if with_pool:
        buf_ref, hist_ref, act_ref, xprev_ref, pbuf_ref, phist_ref = refs[2 * n_convert + 1:]
    else:
        buf_ref, hist_ref, act_ref, xprev_ref = refs[2 * n_convert + 1:]
    t = x_ref.shape[0]
    slabs = FF_CHUNK // LANES
    i = pl.program_id(0)
    cur = i % 2
    prev_tile = (i + tiles_per_seq - 1) % tiles_per_seq

    @pl.when(i == 0)
    def _():
        act_ref[...] = jnp.zeros_like(act_ref)
        xprev_ref[...] = jnp.zeros_like(xprev_ref)

    if with_pool:
        @pl.when(jnp.logical_or(i == 0, prev_tile == 0))
        def _():
            phist_ref[...] = jnp.zeros_like(phist_ref)

    @pl.when(i % tiles_per_seq == 0)
    def _():
        for slab in range(2 * D_FF // LANES):
            lanes = slice(slab * LANES, (slab + 1) * LANES)
            hist_ref[slab] = jnp.broadcast_to(-bup_ref[:, lanes], (SUBLANES, LANES))

    def step(project_up):
        x = x_ref[...]
        xb = x.astype(BF16)

        def chunk_cols(c):
            return (slice(c * FF_CHUNK, (c + 1) * FF_CHUNK),
                    slice(D_FF + c * FF_CHUNK, D_FF + (c + 1) * FF_CHUNK))

        def up_project(c, k):
            cols = chunk_cols(c)[k]
            h = _dot(xb, wup_ref[:, cols])
            for s in range(slabs):
                lanes = slice(s * LANES, (s + 1) * LANES)
                slab = cols.start // LANES + s
                stage = buf_ref.at[c % 2, k, s]
                stage[0:SUBLANES, :] = hist_ref[slab]
                stage[SUBLANES:SUBLANES + t, :] = h[:, lanes]
                hist_ref[slab] = h[t - SUBLANES:, lanes]

        def conv_and_gate(c, s):
            ys = []
            for k, cols in enumerate(chunk_cols(c)):
                lanes = slice(cols.start + s * LANES, cols.start + (s + 1) * LANES)
                w = cw_ref[:, lanes]
                const = cb_ref[:, lanes] + bup_ref[:, lanes] * (w[0:1, :] + w[1:2, :] + w[2:3, :])
                stage = buf_ref.at[c % 2, k, s]
                h0 = stage[SUBLANES:SUBLANES + t, :]
                h1 = stage[SUBLANES - 1:SUBLANES - 1 + t, :]
                h2 = stage[SUBLANES - 2:SUBLANES - 2 + t, :]
                ys.append(w[0:1, :] * h2 + w[1:2, :] * h1 + w[2:3, :] * h0 + const)
            lanes = slice(c * FF_CHUNK + s * LANES, c * FF_CHUNK + (s + 1) * LANES)
            act_ref[cur, :, lanes] = (_gelu_tanh(ys[0]) * ys[1]).astype(BF16)

        def finish_previous_rows(r):
            rows = slice(r, r + DOWN_ROWS)
            f = _dot(act_ref[1 - cur, rows, :], wdn_ref[...])
            y = _layer_norm(ALPHA * xprev_ref[rows, :] + f, g_ref[...], b_ref[...])
            if with_pool:
                pool_stage(r, y)
            else:
                o_ref[rows, :] = y

        pieces = [partial(finish_previous_rows, r) for r in range(0, t, DOWN_ROWS)]
        if with_pool:
            pool_stage, pool_group, pool_finish = _pool_mixer_steps(
                prev_tile, pw_ref, pscale_ref, pg_ref, pb_ref, pbuf_ref, phist_ref, o_ref)
            pieces += [partial(pool_group, gi) for gi in reversed(range(C_GROUPS))] + [pool_finish]
        plan = {}
        for c, piece in zip(POOL_PLAN if with_pool else DOWN_PLAN, pieces, strict=True):
            plan.setdefault(c, []).append(piece)
        if not project_up:
            for c in sorted(plan):
                for piece in plan[c]:
                    piece()
            return
        up_project(0, 0)
        up_project(0, 1)
        for c in range(N_FF_CHUNKS):
            more = c + 1 < N_FF_CHUNKS
            if more:
                up_project(c + 1, 0)
            conv_and_gate(c, 0)
            if more:
                up_project(c + 1, 1)
            for piece in plan.get(c, ()):
                piece()
            conv_and_gate(c, 1)
        xprev_ref[...] = x

    pl.when(i < pl.num_programs(0) - 1)(partial(step, True))
    pl.when(i == pl.num_programs(0) - 1)(partial(step, False))


def _layer_spec(stack, layer, single_buffer):
    shape = (None,) + stack.shape[1:]
    index = (layer,) + (0,) * (stack.ndim - 1)
    if single_buffer:
        return pl.BlockSpec(shape, lambda *_: index, pipeline_mode=pl.Buffered(1))
    return pl.BlockSpec(shape, lambda *_: index)


def _rows(v):
    return v.reshape(v.shape[0], 1, v.shape[1])


def _convert_specs(jobs, step_of):
    in_specs, out_specs, out_shapes, operands = [], [], [], []
    for stack, layer in jobs:
        _, r, c = stack.shape
        n_blocks = max(n for n in range(1, CONVERT_BLOCKS + 1)
                       if r % n == 0 and (r // n) % BF16_ROWS == 0)
        rows = r // n_blocks
        block = lambda *idx, n=n_blocks: jnp.minimum(step_of(*idx), n - 1)
        in_specs.append(pl.BlockSpec((None, rows, c), lambda *idx, l=layer, b=block: (l, b(*idx), 0)))
        out_specs.append(pl.BlockSpec((rows, c), lambda *idx, b=block: (b(*idx), 0)))
        out_shapes.append(jax.ShapeDtypeStruct((r, c), BF16))
        operands.append(stack)
    return in_specs, out_specs, out_shapes, operands


def _mixer_even(x, i, layer, w, convert):
    params = [
        (w["w_in"], 0, True),
        (w["gmlp_ws"], i, False),
        (w["gmlp_bs"], i, False),
        (w["gmlp_ln_g"], i, False), (w["gmlp_ln_b"], i, False),
        (w["sconv_w"], i, False),
        (w["w_out"], 0, True),
        (w["ln_mix_g"], layer, False), (w["ln_mix_b"], layer, False),
    ]
    scratch = [pltpu.VMEM((D_B // LANES, SUBLANES + SEQ_TILE, LANES), F32),
               pltpu.VMEM((D_B // LANES, SUBLANES, LANES), F32)]
    batch, seq, d = x.shape
    tiles_per_seq = seq // SEQ_TILE
    x_spec = pl.BlockSpec((None, SEQ_TILE, d), lambda b, j: (b, j, 0))
    c_in, c_out, c_shapes, c_ops = _convert_specs(convert, lambda b, j: b * tiles_per_seq + j)
    out, *converted = pl.pallas_call(
        partial(_mixer_even_kernel, len(convert)),
        name="mixer_even",
        grid=(batch, tiles_per_seq),
        in_specs=[x_spec] + [_layer_spec(*p) for p in params] + c_in,
        out_specs=[x_spec] + c_out,
        out_shape=[jax.ShapeDtypeStruct(x.shape, x.dtype)] + c_shapes,
        scratch_shapes=scratch,
        compiler_params=pltpu.CompilerParams(
            dimension_semantics=("arbitrary", "arbitrary"),
            vmem_limit_bytes=VMEM_LIMIT_BYTES),
    )(x, *[p[0] for p in params], *c_ops)
    return out, converted


def _ffn(x, layer, w, convert, pool_layer=None):
    params = [
        (w["ffn_w_up"], 0, True),
        (w["ffn_b_up"], layer, False),
        (w["ffn_conv_w"], layer, False),
        (w["ffn_conv_b"], layer, False),
        (w["ffn_w_down"], 0, True),
        (w["ln_ffn_g"], layer, False), (w["ln_ffn_b"], layer, False),
    ]
    slabs = FF_CHUNK // LANES
    scratch = [pltpu.VMEM((2, 2, slabs, SUBLANES + SEQ_TILE, LANES), F32),
               pltpu.VMEM((2 * D_FF // LANES, SUBLANES, LANES), F32),
               pltpu.VMEM((2, SEQ_TILE, D_FF), BF16),
               pltpu.VMEM((SEQ_TILE, D_MODEL), F32)]
    with_pool = pool_layer is not None
    if with_pool:
        params += [(w["pool_w"], pool_layer // 2, False), (w["pool_scale"], pool_layer // 2, False),
                   (w["ln_mix_g"], pool_layer, False), (w["ln_mix_b"], pool_layer, False)]
        scratch += [pltpu.VMEM((D_MODEL // LANES, POOL_HIST + SEQ_TILE, LANES), F32),
                    pltpu.VMEM((D_MODEL // LANES, POOL_HIST, LANES), F32)]
    batch, seq, d = x.shape
    n_tiles = batch * seq // SEQ_TILE
    c_in, c_out, c_shapes, c_ops = _convert_specs(convert, lambda i: i)
    out, *converted = pl.pallas_call(
        partial(_ffn_kernel, seq // SEQ_TILE, with_pool, len(convert)),
        name="conv_ffn",
        grid=(n_tiles + 1,),
        in_specs=[pl.BlockSpec((SEQ_TILE, d), lambda i: (jnp.minimum(i, n_tiles - 1), 0))]
        + [_layer_spec(*p) for p in params] + c_in,
        out_specs=[pl.BlockSpec((SEQ_TILE, d), lambda i: (jnp.maximum(i - 1, 0), 0))] + c_out,
        out_shape=[jax.ShapeDtypeStruct((batch * seq, d), x.dtype)] + c_shapes,
        scratch_shapes=scratch,
        compiler_params=pltpu.CompilerParams(
            dimension_semantics=("arbitrary",),
            vmem_limit_bytes=VMEM_LIMIT_BYTES),
    )(x.reshape(batch * seq, d), *[p[0] for p in params], *c_ops)
    return out.reshape(batch, seq, d), converted


def kernel(x, w_in_even, gmlp_ws, gmlp_bs, gmlp_ln_g, gmlp_ln_b, sconv_w, w_out_even, pool_w, pool_scale, ffn_w_up, ffn_b_up, ffn_conv_w, ffn_conv_b, ffn_w_down, ln_mix_g, ln_mix_b, ln_ffn_g, ln_ffn_b):
    assert x.shape[1] % SEQ_TILE == 0 and SEQ_TILE % GMLP_BLOCK == 0
    assert x.shape[0] * x.shape[1] // SEQ_TILE >= CONVERT_BLOCKS
    w = dict(
        gmlp_ws=gmlp_ws,
        gmlp_bs=jnp.broadcast_to(gmlp_bs[..., None], gmlp_bs.shape + (A_HEAD_DIM,)),
        gmlp_ln_g=_rows(gmlp_ln_g), gmlp_ln_b=_rows(gmlp_ln_b), sconv_w=sconv_w,
        pool_w=pool_w.astype(BF16), pool_scale=_rows(pool_scale),
        ffn_b_up=_rows(ffn_b_up), ffn_conv_w=ffn_conv_w, ffn_conv_b=_rows(ffn_conv_b),
        ln_mix_g=_rows(ln_mix_g), ln_mix_b=_rows(ln_mix_b),
        ln_ffn_g=_rows(ln_ffn_g), ln_ffn_b=_rows(ln_ffn_b),
    )
    ffn_jobs = lambda layer: [(ffn_w_up, layer), (ffn_w_down, layer)]

    def use_ffn(converted):
        w["ffn_w_up"], w["ffn_w_down"] = converted[0][None], converted[1][None]

    w["w_in"], w["w_out"] = w_in_even[:1].astype(BF16), w_out_even[:1].astype(BF16)
    for layer in range(0, DEPTH, 2):
        x, converted = _mixer_even(x, layer // 2, layer, w, ffn_jobs(layer))
        use_ffn(converted)
        x, converted = _ffn(x, layer, w, ffn_jobs(layer + 1), pool_layer=layer + 1)
        use_ffn(converted)
        more = layer + 2 < DEPTH
        jobs = [(w_in_even, layer // 2 + 1), (w_out_even, layer // 2 + 1)] if more else []
        x, converted = _ffn(x, layer + 1, w, jobs)
        if more:
            w["w_in"], w["w_out"] = converted[0][None], converted[1][None]
    return x
```

```python
from functools import partial

import jax
import jax.numpy as jnp
from jax import lax
from jax.experimental import pallas as pl
from jax.experimental.pallas import tpu as pltpu

D_MODEL = 1024
DEPTH = 4
CHUNK = 64
GMLP_BLOCK = 128
A_HEADS = 4
A_HEAD_DIM = 128
D_A = A_HEADS * A_HEAD_DIM
D_B = 512
B_CONV = 3
C_WINDOWS = (2, 4, 8, 16)
C_GROUPS = len(C_WINDOWS)
D_C = D_MODEL // C_GROUPS
D_FF = 2816
FFN_CONV = 3
ALPHA = (2.0 * DEPTH) ** 0.25
LN_EPS = 1e-5

LANES = 128
SUBLANES = 8
BF16_ROWS = 16
SEQ_TILE = 512
FF_CHUNK = 256
N_FF_CHUNKS = D_FF // FF_CHUNK
DOWN_ROWS = 256
DOWN_PLAN = (3, 8)
CONVERT_BLOCKS = 32
OUT_ROWS = 256
POOL_HIST = 16
VMEM_LIMIT_BYTES = 56 * 1024 * 1024

BF16 = jnp.bfloat16
F32 = jnp.float32


def _layer_norm(z, g, b):
    mu = jnp.mean(z, axis=-1, keepdims=True)
    zc = z - mu
    var = jnp.mean(zc * zc, axis=-1, keepdims=True)
    return zc * lax.rsqrt(var + LN_EPS) * g + b


def _dot(a, b):
    return jnp.dot(a, b, preferred_element_type=F32)


def _convert_blocks(src_refs, dst_refs):
    for src, dst in zip(src_refs, dst_refs, strict=True):
        dst[...] = src[...].astype(BF16)


def _causal_conv3(buf_ref, hist_ref, h, w):
    t, c = h.shape
    out = []
    for s in range(c // LANES):
        lanes = slice(s * LANES, (s + 1) * LANES)
        hs = h[:, lanes]
        buf_ref[s, 0:SUBLANES, :] = hist_ref[s]
        buf_ref[s, SUBLANES:SUBLANES + t, :] = hs
        hist_ref[s] = hs[t - SUBLANES:, :]
        h1 = buf_ref[s, SUBLANES - 1:SUBLANES - 1 + t, :]
        h2 = buf_ref[s, SUBLANES - 2:SUBLANES - 2 + t, :]
        out.append(w[0:1, lanes] * h2 + w[1:2, lanes] * h1 + w[2:3, lanes] * hs)
    return jnp.concatenate(out, axis=-1)


def _mixer_even_kernel(n_convert, x_ref, win_ref, ws_ref, bsb_ref, lng_ref, lnb_ref, cw_ref,
                       wout_ref, g_ref, b_ref, *refs):
    o_ref, buf_ref, hist_ref = refs[n_convert], refs[-2], refs[-1]

    @pl.when(pl.program_id(1) == 0)
    def _():
        hist_ref[...] = jnp.zeros_like(hist_ref)

    x = x_ref[...]
    t = x.shape[0]
    xb = x.astype(BF16)

    uv = _dot(xb, win_ref[:, :2 * D_A])
    bch = _dot(xb, win_ref[:, 2 * D_A:])

    uv = jax.nn.gelu(uv)
    row = lax.broadcasted_iota(jnp.int32, (GMLP_BLOCK, GMLP_BLOCK), 0) // CHUNK
    col = lax.broadcasted_iota(jnp.int32, (GMLP_BLOCK, GMLP_BLOCK), 1) // CHUNK
    mask = col <= row
    ya = []
    for h in range(A_HEADS):
        lanes = slice(h * A_HEAD_DIM, (h + 1) * A_HEAD_DIM)
        u = uv[:, lanes]
        v = uv[:, D_A + h * A_HEAD_DIM:D_A + (h + 1) * A_HEAD_DIM]
        vn = _layer_norm(v, lng_ref[:, lanes], lnb_ref[:, lanes]).astype(BF16)
        wm = jnp.where(mask, ws_ref[h], 0.0).astype(BF16)
        s = [_dot(wm, vn[n * GMLP_BLOCK:(n + 1) * GMLP_BLOCK, :]) + bsb_ref[h]
             for n in range(t // GMLP_BLOCK)]
        ya.append(u * jnp.concatenate(s, axis=0))

    gb = bch[:, :D_B]
    ch = bch[:, D_B:2 * D_B] * bch[:, 2 * D_B:]
    yb = gb * _causal_conv3(buf_ref, hist_ref, ch, cw_ref[...])

    ycat = jnp.concatenate(ya + [yb], axis=-1).astype(BF16)
    for r in range(0, t, OUT_ROWS):
        rows = slice(r, r + OUT_ROWS)
        y = _dot(ycat[rows], wout_ref[...])
        o_ref[rows, :] = _layer_norm(ALPHA * x[rows] + y, g_ref[...], b_ref[...])
    _convert_blocks(refs[:n_convert], refs[n_convert + 1:-2])


def _mixer_odd_kernel(x_ref, pw_ref, scale_ref, g_ref, b_ref, o_ref, buf_ref, hist_ref):
    j = pl.program_id(1)

    @pl.when(j == 0)
    def _():
        hist_ref[...] = jnp.zeros_like(hist_ref)

    x = x_ref[...]
    t = x.shape[0]
    rows = slice(POOL_HIST, POOL_HIST + t)
    for s in range(D_MODEL // LANES):
        lanes = slice(s * LANES, (s + 1) * LANES)
        buf_ref[s, 0:POOL_HIST, :] = hist_ref[s]
        buf_ref[s, rows, :] = x[:, lanes]
        hist_ref[s] = x[t - POOL_HIST:, lanes]

    pos = j * t + lax.broadcasted_iota(jnp.int32, (t, 1), 0) + 1
    ys = []
    for gi, win in enumerate(C_WINDOWS):
        cnt = jnp.minimum(pos, win).astype(F32)
        p = []
        for s in range(gi * D_C // LANES, (gi + 1) * D_C // LANES):
            xs = x[:, s * LANES:(s + 1) * LANES]
            acc = xs
            for k in range(1, win):
                acc = acc + buf_ref[s, POOL_HIST - k:POOL_HIST - k + t, :]
            p.append(acc / cnt - xs)
        ys.append(_dot(jnp.concatenate(p, axis=-1).astype(BF16), pw_ref[gi]))
    y = jnp.concatenate(ys, axis=-1) * scale_ref[...]
    o_ref[...] = _layer_norm(ALPHA * x + y, g_ref[...], b_ref[...])


def _gelu_tanh(x):
    k = (2.0 / jnp.pi) ** 0.5
    return x * (0.5 + 0.5 * jnp.tanh(x * (k + (0.044715 * k) * (x * x))))


def _ffn_kernel(tiles_per_seq, n_convert, x_ref, wup_ref, bup_ref, cw_ref, cb_ref, wdn_ref, g_ref,
                b_ref, *refs):
    o_ref = refs[n_convert]
    buf_ref, hist_ref, act_ref, xprev_ref = refs[2 * n_convert + 1:]
    t = x_ref.shape[0]
    slabs = FF_CHUNK // LANES
    i = pl.program_id(0)
    cur = i % 2

    @pl.when(i == 0)
    def _():
        act_ref[...] = jnp.zeros_like(act_ref)
        xprev_ref[...] = jnp.zeros_like(xprev_ref)

    @pl.when(i % tiles_per_seq == 0)
    def _():
        for slab in range(2 * D_FF // LANES):
            lanes = slice(slab * LANES, (slab + 1) * LANES)
            hist_ref[slab] = jnp.broadcast_to(-bup_ref[:, lanes], (SUBLANES, LANES))

    def step(project_up):
        x = x_ref[...]
        xb = x.astype(BF16)

        def chunk_cols(c):
            return (slice(c * FF_CHUNK, (c + 1) * FF_CHUNK),
                    slice(D_FF + c * FF_CHUNK, D_FF + (c + 1) * FF_CHUNK))

        def up_project(c, k):
            cols = chunk_cols(c)[k]
            h = _dot(xb, wup_ref[:, cols])
            for s in range(slabs):
                lanes = slice(s * LANES, (s + 1) * LANES)
                slab = cols.start // LANES + s
                stage = buf_ref.at[c % 2, k, s]
                stage[0:SUBLANES, :] = hist_ref[slab]
                stage[SUBLANES:SUBLANES + t, :] = h[:, lanes]
                hist_ref[slab] = h[t - SUBLANES:, lanes]

        def conv_and_gate(c, s):
            ys = []
            for k, cols in enumerate(chunk_cols(c)):
                lanes = slice(cols.start + s * LANES, cols.start + (s + 1) * LANES)
                w = cw_ref[:, lanes]
                const = cb_ref[:, lanes] + bup_ref[:, lanes] * (w[0:1, :] + w[1:2, :] + w[2:3, :])
                stage = buf_ref.at[c % 2, k, s]
                h0 = stage[SUBLANES:SUBLANES + t, :]
                h1 = stage[SUBLANES - 1:SUBLANES - 1 + t, :]
                h2 = stage[SUBLANES - 2:SUBLANES - 2 + t, :]
                ys.append(w[0:1, :] * h2 + w[1:2, :] * h1 + w[2:3, :] * h0 + const)
            lanes = slice(c * FF_CHUNK + s * LANES, c * FF_CHUNK + (s + 1) * LANES)
            act_ref[cur, :, lanes] = (_gelu_tanh(ys[0]) * ys[1]).astype(BF16)

        def finish_previous_rows(r):
            rows = slice(r, r + DOWN_ROWS)
            f = _dot(act_ref[1 - cur, rows, :], wdn_ref[...])
            o_ref[rows, :] = _layer_norm(ALPHA * xprev_ref[rows, :] + f, g_ref[...], b_ref[...])

        plan = dict(zip(DOWN_PLAN, range(0, t, DOWN_ROWS), strict=True))
        if not project_up:
            for r in plan.values():
                finish_previous_rows(r)
            return
        up_project(0, 0)
        up_project(0, 1)
        for c in range(N_FF_CHUNKS):
            more = c + 1 < N_FF_CHUNKS
            if more:
                up_project(c + 1, 0)
            conv_and_gate(c, 0)
            if more:
                up_project(c + 1, 1)
            if c in plan:
                finish_previous_rows(plan[c])
            conv_and_gate(c, 1)
        xprev_ref[...] = x
        _convert_blocks(refs[:n_convert], refs[n_convert + 1:2 * n_convert + 1])

    pl.when(i < pl.num_programs(0) - 1)(partial(step, True))
    pl.when(i == pl.num_programs(0) - 1)(partial(step, False))


def _layer_spec(stack, layer, single_buffer):
    shape = (None,) + stack.shape[1:]
    index = (layer,) + (0,) * (stack.ndim - 1)
    if single_buffer:
        return pl.BlockSpec(shape, lambda *_: index, pipeline_mode=pl.Buffered(1))
    return pl.BlockSpec(shape, lambda *_: index)


def _rows(v):
    return v.reshape(v.shape[0], 1, v.shape[1])


def _convert_specs(jobs, step_of):
    in_specs, out_specs, out_shapes, operands = [], [], [], []
    for stack, layer in jobs:
        _, r, c = stack.shape
        n_blocks = max(n for n in range(1, CONVERT_BLOCKS + 1)
                       if r % n == 0 and (r // n) % BF16_ROWS == 0)
        rows = r // n_blocks
        block = lambda *idx, n=n_blocks: jnp.minimum(step_of(*idx), n - 1)
        in_specs.append(pl.BlockSpec((None, rows, c), lambda *idx, l=layer, b=block: (l, b(*idx), 0)))
        out_specs.append(pl.BlockSpec((rows, c), lambda *idx, b=block: (b(*idx), 0)))
        out_shapes.append(jax.ShapeDtypeStruct((r, c), BF16))
        operands.append(stack)
    return in_specs, out_specs, out_shapes, operands


def _mixer_even(x, i, layer, w, convert):
    params = [
        (w["w_in"], 0, True),
        (w["gmlp_ws"], i, False),
        (w["gmlp_bs"], i, False),
        (w["gmlp_ln_g"], i, False), (w["gmlp_ln_b"], i, False),
        (w["sconv_w"], i, False),
        (w["w_out"], 0, True),
        (w["ln_mix_g"], layer, False), (w["ln_mix_b"], layer, False),
    ]
    scratch = [pltpu.VMEM((D_B // LANES, SUBLANES + SEQ_TILE, LANES), F32),
               pltpu.VMEM((D_B // LANES, SUBLANES, LANES), F32)]
    batch, seq, d = x.shape
    tiles_per_seq = seq // SEQ_TILE
    x_spec = pl.BlockSpec((None, SEQ_TILE, d), lambda b, j: (b, j, 0))
    c_in, c_out, c_shapes, c_ops = _convert_specs(convert, lambda b, j: b * tiles_per_seq + j)
    out, *converted = pl.pallas_call(
        partial(_mixer_even_kernel, len(convert)),
        name="mixer_even",
        grid=(batch, tiles_per_seq),
        in_specs=[x_spec] + [_layer_spec(*p) for p in params] + c_in,
        out_specs=[x_spec] + c_out,
        out_shape=[jax.ShapeDtypeStruct(x.shape, x.dtype)] + c_shapes,
        scratch_shapes=scratch,
        compiler_params=pltpu.CompilerParams(
            dimension_semantics=("arbitrary", "arbitrary"),
            vmem_limit_bytes=VMEM_LIMIT_BYTES),
    )(x, *[p[0] for p in params], *c_ops)
    return out, converted


def _mixer_odd(x, i, layer, w):
    params = [(w["pool_w"], i, False), (w["pool_scale"], i, False),
              (w["ln_mix_g"], layer, False), (w["ln_mix_b"], layer, False)]
    batch, seq, d = x.shape
    x_spec = pl.BlockSpec((None, SEQ_TILE, d), lambda b, j: (b, j, 0))
    return pl.pallas_call(
        _mixer_odd_kernel,
        name="mixer_odd",
        grid=(batch, seq // SEQ_TILE),
        in_specs=[x_spec] + [_layer_spec(*p) for p in params],
        out_specs=x_spec,
        out_shape=jax.ShapeDtypeStruct(x.shape, x.dtype),
        scratch_shapes=[pltpu.VMEM((D_MODEL // LANES, POOL_HIST + SEQ_TILE, LANES), F32),
                        pltpu.VMEM((D_MODEL // LANES, POOL_HIST, LANES), F32)],
        compiler_params=pltpu.CompilerParams(
            dimension_semantics=("arbitrary", "arbitrary"),
            vmem_limit_bytes=VMEM_LIMIT_BYTES),
    )(x, *[p[0] for p in params])


def _ffn(x, layer, w, convert):
    params = [
        (w["ffn_w_up"], 0, True),
        (w["ffn_b_up"], layer, False),
        (w["ffn_conv_w"], layer, False),
        (w["ffn_conv_b"], layer, False),
        (w["ffn_w_down"], 0, True),
        (w["ln_ffn_g"], layer, False), (w["ln_ffn_b"], layer, False),
    ]
    slabs = FF_CHUNK // LANES
    scratch = [pltpu.VMEM((2, 2, slabs, SUBLANES + SEQ_TILE, LANES), F32),
               pltpu.VMEM((2 * D_FF // LANES, SUBLANES, LANES), F32),
               pltpu.VMEM((2, SEQ_TILE, D_FF), BF16),
               pltpu.VMEM((SEQ_TILE, D_MODEL), F32)]
    batch, seq, d = x.shape
    n_tiles = batch * seq // SEQ_TILE
    c_in, c_out, c_shapes, c_ops = _convert_specs(convert, lambda i: i)
    out, *converted = pl.pallas_call(
        partial(_ffn_kernel, seq // SEQ_TILE, len(convert)),
        name="conv_ffn",
        grid=(n_tiles + 1,),
        in_specs=[pl.BlockSpec((SEQ_TILE, d), lambda i: (jnp.minimum(i, n_tiles - 1), 0))]
        + [_layer_spec(*p) for p in params] + c_in,
        out_specs=[pl.BlockSpec((SEQ_TILE, d), lambda i: (jnp.maximum(i - 1, 0), 0))] + c_out,
        out_shape=[jax.ShapeDtypeStruct((batch * seq, d), x.dtype)] + c_shapes,
        scratch_shapes=scratch,
        compiler_params=pltpu.CompilerParams(
            dimension_semantics=("arbitrary",),
            vmem_limit_bytes=VMEM_LIMIT_BYTES),
    )(x.reshape(batch * seq, d), *[p[0] for p in params], *c_ops)
    return out.reshape(batch, seq, d), converted


def kernel(x, w_in_even, gmlp_ws, gmlp_bs, gmlp_ln_g, gmlp_ln_b, sconv_w, w_out_even, pool_w, pool_scale, ffn_w_up, ffn_b_up, ffn_conv_w, ffn_conv_b, ffn_w_down, ln_mix_g, ln_mix_b, ln_ffn_g, ln_ffn_b):
    assert x.shape[1] % SEQ_TILE == 0 and SEQ_TILE % GMLP_BLOCK == 0
    assert x.shape[0] * x.shape[1] // SEQ_TILE >= CONVERT_BLOCKS
    w = dict(
        gmlp_ws=gmlp_ws,
        gmlp_bs=jnp.broadcast_to(gmlp_bs[..., None], gmlp_bs.shape + (A_HEAD_DIM,)),
        gmlp_ln_g=_rows(gmlp_ln_g), gmlp_ln_b=_rows(gmlp_ln_b), sconv_w=sconv_w,
        pool_w=pool_w.astype(BF16), pool_scale=_rows(pool_scale),
        ffn_b_up=_rows(ffn_b_up), ffn_conv_w=ffn_conv_w, ffn_conv_b=_rows(ffn_conv_b),
        ln_mix_g=_rows(ln_mix_g), ln_mix_b=_rows(ln_mix_b),
        ln_ffn_g=_rows(ln_ffn_g), ln_ffn_b=_rows(ln_ffn_b),
    )
    ffn_jobs = lambda layer: [(ffn_w_up, layer), (ffn_w_down, layer)]

    def use_ffn(converted):
        w["ffn_w_up"], w["ffn_w_down"] = converted[0][None], converted[1][None]

    w["w_in"], w["w_out"] = w_in_even[:1].astype(BF16), w_out_even[:1].astype(BF16)
    for layer in range(0, DEPTH, 2):
        x, converted = _mixer_even(x, layer // 2, layer, w, ffn_jobs(layer))
        use_ffn(converted)
        x, converted = _ffn(x, layer, w, ffn_jobs(layer + 1))
        use_ffn(converted)
        x = _mixer_odd(x, layer // 2, layer + 1, w)
        more = layer + 2 < DEPTH
        jobs = [(w_in_even, layer // 2 + 1), (w_out_even, layer // 2 + 1)] if more else []
        x, converted = _ffn(x, layer + 1, w, jobs)
        if more:
            w["w_in"], w["w_out"] = converted[0][None], converted[1][None]
    return x
```

```python
from functools import partial

import jax
import jax.numpy as jnp
from jax import lax
from jax.experimental import pallas as pl
from jax.experimental.pallas import tpu as pltpu

D_MODEL = 1024
DEPTH = 4
CHUNK = 64
GMLP_BLOCK = 128
A_HEADS = 4
A_HEAD_DIM = 128
D_A = A_HEADS * A_HEAD_DIM
D_B = 512
B_CONV = 3
C_WINDOWS = (2, 4, 8, 16)
C_GROUPS = len(C_WINDOWS)
D_C = D_MODEL // C_GROUPS
D_FF = 2816
FFN_CONV = 3
ALPHA = (2.0 * DEPTH) ** 0.25
LN_EPS = 1e-5

LANES = 128
SUBLANES = 8
BF16_ROWS = 16
SEQ_TILE = 512
FF_CHUNK = 256
N_FF_CHUNKS = D_FF // FF_CHUNK
DOWN_ROWS = 256
DOWN_PLAN = (3, 8)
CONVERT_BLOCKS = 32
OUT_ROWS = 256
POOL_TILE = 1024
POOL_QUAD_ROWS = 16
POOL_HIST = 24
VMEM_LIMIT_BYTES = 56 * 1024 * 1024

BF16 = jnp.bfloat16
F32 = jnp.float32


def _layer_norm(z, g, b):
    mu = jnp.mean(z, axis=-1, keepdims=True)
    zc = z - mu
    var = jnp.mean(zc * zc, axis=-1, keepdims=True)
    return zc * lax.rsqrt(var + LN_EPS) * g + b


def _dot(a, b):
    return jnp.dot(a, b, preferred_element_type=F32)


def _convert_blocks(src_refs, dst_refs):
    for src, dst in zip(src_refs, dst_refs, strict=True):
        dst[...] = src[...].astype(BF16)


def _causal_conv3(buf_ref, hist_ref, h, w):
    t, c = h.shape
    out = []
    for s in range(c // LANES):
        lanes = slice(s * LANES, (s + 1) * LANES)
        hs = h[:, lanes]
        buf_ref[s, 0:SUBLANES, :] = hist_ref[s]
        buf_ref[s, SUBLANES:SUBLANES + t, :] = hs
        hist_ref[s] = hs[t - SUBLANES:, :]
        h1 = buf_ref[s, SUBLANES - 1:SUBLANES - 1 + t, :]
        h2 = buf_ref[s, SUBLANES - 2:SUBLANES - 2 + t, :]
        out.append(w[0:1, lanes] * h2 + w[1:2, lanes] * h1 + w[2:3, lanes] * hs)
    return jnp.concatenate(out, axis=-1)


def _mixer_even_kernel(n_convert, x_ref, win_ref, ws_ref, bsb_ref, lng_ref, lnb_ref, cw_ref,
                       wout_ref, g_ref, b_ref, *refs):
    o_ref, buf_ref, hist_ref = refs[n_convert], refs[-2], refs[-1]

    @pl.when(pl.program_id(1) == 0)
    def _():
        hist_ref[...] = jnp.zeros_like(hist_ref)

    x = x_ref[...]
    t = x.shape[0]
    xb = x.astype(BF16)

    uv = _dot(xb, win_ref[:, :2 * D_A])
    bch = _dot(xb, win_ref[:, 2 * D_A:])

    uv = jax.nn.gelu(uv)
    row = lax.broadcasted_iota(jnp.int32, (GMLP_BLOCK, GMLP_BLOCK), 0) // CHUNK
    col = lax.broadcasted_iota(jnp.int32, (GMLP_BLOCK, GMLP_BLOCK), 1) // CHUNK
    mask = col <= row
    ya = []
    for h in range(A_HEADS):
        lanes = slice(h * A_HEAD_DIM, (h + 1) * A_HEAD_DIM)
        u = uv[:, lanes]
        v = uv[:, D_A + h * A_HEAD_DIM:D_A + (h + 1) * A_HEAD_DIM]
        vn = _layer_norm(v, lng_ref[:, lanes], lnb_ref[:, lanes]).astype(BF16)
        wm = jnp.where(mask, ws_ref[h], 0.0).astype(BF16)
        s = [_dot(wm, vn[n * GMLP_BLOCK:(n + 1) * GMLP_BLOCK, :]) + bsb_ref[h]
             for n in range(t // GMLP_BLOCK)]
        ya.append(u * jnp.concatenate(s, axis=0))

    gb = bch[:, :D_B]
    ch = bch[:, D_B:2 * D_B] * bch[:, 2 * D_B:]
    yb = gb * _causal_conv3(buf_ref, hist_ref, ch, cw_ref[...])

    ycat = jnp.concatenate(ya + [yb], axis=-1).astype(BF16)
    for r in range(0, t, OUT_ROWS):
        rows = slice(r, r + OUT_ROWS)
        y = _dot(ycat[rows], wout_ref[...])
        o_ref[rows, :] = _layer_norm(ALPHA * x[rows] + y, g_ref[...], b_ref[...])
    _convert_blocks(refs[:n_convert], refs[n_convert + 1:-2])


def _mixer_odd_kernel(x_ref, pw_ref, scale_ref, g_ref, b_ref, o_ref, buf_ref, hist_ref, sum_ref):
    j = pl.program_id(1)

    @pl.when(j == 0)
    def _():
        hist_ref[...] = jnp.zeros_like(hist_ref)

    x = x_ref[...]
    t = x.shape[0]
    for s in range(D_MODEL // LANES):
        lanes = slice(s * LANES, (s + 1) * LANES)
        buf_ref[s, 0:POOL_HIST, :] = hist_ref[s]
        buf_ref[s, POOL_HIST:POOL_HIST + t, :] = x[:, lanes]
        hist_ref[s] = x[t - POOL_HIST:, lanes]

    def trailing(s, back, rows, first=0):
        return buf_ref[s, POOL_HIST + first - back:POOL_HIST + first - back + rows, :]

    pos = j * t + lax.broadcasted_iota(jnp.int32, (t, 1), 0) + 1
    ys = []
    for gi, win in enumerate(C_WINDOWS):
        cnt = jnp.minimum(pos, win).astype(F32)
        p = []
        for s in range(gi * D_C // LANES, (gi + 1) * D_C // LANES):
            xs = x[:, s * LANES:(s + 1) * LANES]
            if win <= 4:
                acc = xs
                for k in range(1, win):
                    acc = acc + trailing(s, k, t)
            else:
                n = POOL_QUAD_ROWS + t
                quad = trailing(s, 0, n, -POOL_QUAD_ROWS)
                for k in range(1, 4):
                    quad = quad + trailing(s, k, n, -POOL_QUAD_ROWS)
                sum_ref[...] = quad
                acc = sum_ref[POOL_QUAD_ROWS:POOL_QUAD_ROWS + t, :]
                for k in range(4, win, 4):
                    acc = acc + sum_ref[POOL_QUAD_ROWS - k:POOL_QUAD_ROWS - k + t, :]
            p.append(acc / cnt - xs)
        ys.append(_dot(jnp.concatenate(p, axis=-1).astype(BF16), pw_ref[gi]))
    y = jnp.concatenate(ys, axis=-1) * scale_ref[...]
    o_ref[...] = _layer_norm(ALPHA * x + y, g_ref[...], b_ref[...])


def _gelu_tanh(x):
    k = (2.0 / jnp.pi) ** 0.5
    return x * (0.5 + 0.5 * jnp.tanh(x * (k + (0.044715 * k) * (x * x))))


def _ffn_kernel(tiles_per_seq, n_convert, x_ref, wup_ref, bup_ref, cw_ref, cb_ref, wdn_ref, g_ref,
                b_ref, *refs):
    o_ref = refs[n_convert]
    buf_ref, hist_ref, act_ref, xprev_ref = refs[2 * n_convert + 1:]
    t = x_ref.shape[0]
    slabs = FF_CHUNK // LANES
    i = pl.program_id(0)
    cur = i % 2

    @pl.when(i == 0)
    def _():
        act_ref[...] = jnp.zeros_like(act_ref)
        xprev_ref[...] = jnp.zeros_like(xprev_ref)

    @pl.when(i % tiles_per_seq == 0)
    def _():
        for slab in range(2 * D_FF // LANES):
            lanes = slice(slab * LANES, (slab + 1) * LANES)
            hist_ref[slab] = jnp.broadcast_to(-bup_ref[:, lanes], (SUBLANES, LANES))

    def step(project_up):
        x = x_ref[...]
        xb = x.astype(BF16)

        def chunk_cols(c):
            return (slice(c * FF_CHUNK, (c + 1) * FF_CHUNK),
                    slice(D_FF + c * FF_CHUNK, D_FF + (c + 1) * FF_CHUNK))

        def up_project(c, k):
            cols = chunk_cols(c)[k]
            h = _dot(xb, wup_ref[:, cols])
            for s in range(slabs):
                lanes = slice(s * LANES, (s + 1) * LANES)
                slab = cols.start // LANES + s
                stage = buf_ref.at[c % 2, k, s]
                stage[0:SUBLANES, :] = hist_ref[slab]
                stage[SUBLANES:SUBLANES + t, :] = h[:, lanes]
                hist_ref[slab] = h[t - SUBLANES:, lanes]

        def conv_and_gate(c, s):
            ys = []
            for k, cols in enumerate(chunk_cols(c)):
                lanes = slice(cols.start + s * LANES, cols.start + (s + 1) * LANES)
                w = cw_ref[:, lanes]
                const = cb_ref[:, lanes] + bup_ref[:, lanes] * (w[0:1, :] + w[1:2, :] + w[2:3, :])
                stage = buf_ref.at[c % 2, k, s]
                h0 = stage[SUBLANES:SUBLANES + t, :]
                h1 = stage[SUBLANES - 1:SUBLANES - 1 + t, :]
                h2 = stage[SUBLANES - 2:SUBLANES - 2 + t, :]
                ys.append(w[0:1, :] * h2 + w[1:2, :] * h1 + w[2:3, :] * h0 + const)
            lanes = slice(c * FF_CHUNK + s * LANES, c * FF_CHUNK + (s + 1) * LANES)
            act_ref[cur, :, lanes] = (_gelu_tanh(ys[0]) * ys[1]).astype(BF16)

        def finish_previous_rows(r):
            rows = slice(r, r + DOWN_ROWS)
            f = _dot(act_ref[1 - cur, rows, :], wdn_ref[...])
            o_ref[rows, :] = _layer_norm(ALPHA * xprev_ref[rows, :] + f, g_ref[...], b_ref[...])

        plan = dict(zip(DOWN_PLAN, range(0, t, DOWN_ROWS), strict=True))
        if not project_up:
            for r in plan.values():
                finish_previous_rows(r)
            return
        up_project(0, 0)
        up_project(0, 1)
        for c in range(N_FF_CHUNKS):
            more = c + 1 < N_FF_CHUNKS
            if more:
                up_project(c + 1, 0)
            conv_and_gate(c, 0)
            if more:
                up_project(c + 1, 1)
            if c in plan:
                finish_previous_rows(plan[c])
            conv_and_gate(c, 1)
        xprev_ref[...] = x
        _convert_blocks(refs[:n_convert], refs[n_convert + 1:2 * n_convert + 1])

    pl.when(i < pl.num_programs(0) - 1)(partial(step, True))
    pl.when(i == pl.num_programs(0) - 1)(partial(step, False))


def _layer_spec(stack, layer, single_buffer):
    shape = (None,) + stack.shape[1:]
    index = (layer,) + (0,) * (stack.ndim - 1)
    if single_buffer:
        return pl.BlockSpec(shape, lambda *_: index, pipeline_mode=pl.Buffered(1))
    return pl.BlockSpec(shape, lambda *_: index)


def _rows(v):
    return v.reshape(v.shape[0], 1, v.shape[1])


def _convert_specs(jobs, step_of):
    in_specs, out_specs, out_shapes, operands = [], [], [], []
    for stack, layer in jobs:
        _, r, c = stack.shape
        n_blocks = max(n for n in range(1, CONVERT_BLOCKS + 1)
                       if r % n == 0 and (r // n) % BF16_ROWS == 0)
        rows = r // n_blocks
        block = lambda *idx, n=n_blocks: jnp.minimum(step_of(*idx), n - 1)
        in_specs.append(pl.BlockSpec((None, rows, c), lambda *idx, l=layer, b=block: (l, b(*idx), 0)))
        out_specs.append(pl.BlockSpec((rows, c), lambda *idx, b=block: (b(*idx), 0)))
        out_shapes.append(jax.ShapeDtypeStruct((r, c), BF16))
        operands.append(stack)
    return in_specs, out_specs, out_shapes, operands


def _mixer_even(x, i, layer, w, convert):
    params = [
        (w["w_in"], 0, True),
        (w["gmlp_ws"], i, False),
        (w["gmlp_bs"], i, False),
        (w["gmlp_ln_g"], i, False), (w["gmlp_ln_b"], i, False),
        (w["sconv_w"], i, False),
        (w["w_out"], 0, True),
        (w["ln_mix_g"], layer, False), (w["ln_mix_b"], layer, False),
    ]
    scratch = [pltpu.VMEM((D_B // LANES, SUBLANES + SEQ_TILE, LANES), F32),
               pltpu.VMEM((D_B // LANES, SUBLANES, LANES), F32)]
    batch, seq, d = x.shape
    tiles_per_seq = seq // SEQ_TILE
    x_spec = pl.BlockSpec((None, SEQ_TILE, d), lambda b, j: (b, j, 0))
    c_in, c_out, c_shapes, c_ops = _convert_specs(convert, lambda b, j: b * tiles_per_seq + j)
    out, *converted = pl.pallas_call(
        partial(_mixer_even_kernel, len(convert)),
        name="mixer_even",
        grid=(batch, tiles_per_seq),
        in_specs=[x_spec] + [_layer_spec(*p) for p in params] + c_in,
        out_specs=[x_spec] + c_out,
        out_shape=[jax.ShapeDtypeStruct(x.shape, x.dtype)] + c_shapes,
        scratch_shapes=scratch,
        compiler_params=pltpu.CompilerParams(
            dimension_semantics=("arbitrary", "arbitrary"),
            vmem_limit_bytes=VMEM_LIMIT_BYTES),
    )(x, *[p[0] for p in params], *c_ops)
    return out, converted


def _mixer_odd(x, i, layer, w):
    params = [(w["pool_w"], i, False), (w["pool_scale"], i, False),
              (w["ln_mix_g"], layer, False), (w["ln_mix_b"], layer, False)]
    batch, seq, d = x.shape
    x_spec = pl.BlockSpec((None, POOL_TILE, d), lambda b, j: (b, j, 0))
    return pl.pallas_call(
        _mixer_odd_kernel,
        name="mixer_odd",
        grid=(batch, seq // POOL_TILE),
        in_specs=[x_spec] + [_layer_spec(*p) for p in params],
        out_specs=x_spec,
        out_shape=jax.ShapeDtypeStruct(x.shape, x.dtype),
        scratch_shapes=[pltpu.VMEM((D_MODEL // LANES, POOL_HIST + POOL_TILE, LANES), F32),
                        pltpu.VMEM((D_MODEL // LANES, POOL_HIST, LANES), F32),
                        pltpu.VMEM((POOL_QUAD_ROWS + POOL_TILE, LANES), F32)],
        compiler_params=pltpu.CompilerParams(
            dimension_semantics=("arbitrary", "arbitrary"),
            vmem_limit_bytes=VMEM_LIMIT_BYTES),
    )(x, *[p[0] for p in params])


def _ffn(x, layer, w, convert):
    params = [
        (w["ffn_w_up"], 0, True),
        (w["ffn_b_up"], layer, False),
        (w["ffn_conv_w"], layer, False),
        (w["ffn_conv_b"], layer, False),
        (w["ffn_w_down"], 0, True),
        (w["ln_ffn_g"], layer, False), (w["ln_ffn_b"], layer, False),
    ]
    slabs = FF_CHUNK // LANES
    scratch = [pltpu.VMEM((2, 2, slabs, SUBLANES + SEQ_TILE, LANES), F32),
               pltpu.VMEM((2 * D_FF // LANES, SUBLANES, LANES), F32),
               pltpu.VMEM((2, SEQ_TILE, D_FF), BF16),
               pltpu.VMEM((SEQ_TILE, D_MODEL), F32)]
    batch, seq, d = x.shape
    n_tiles = batch * seq // SEQ_TILE
    c_in, c_out, c_shapes, c_ops = _convert_specs(convert, lambda i: i)
    out, *converted = pl.pallas_call(
        partial(_ffn_kernel, seq // SEQ_TILE, len(convert)),
        name="conv_ffn",
        grid=(n_tiles + 1,),
        in_specs=[pl.BlockSpec((SEQ_TILE, d), lambda i: (jnp.minimum(i, n_tiles - 1), 0))]
        + [_layer_spec(*p) for p in params] + c_in,
        out_specs=[pl.BlockSpec((SEQ_TILE, d), lambda i: (jnp.maximum(i - 1, 0), 0))] + c_out,
        out_shape=[jax.ShapeDtypeStruct((batch * seq, d), x.dtype)] + c_shapes,
        scratch_shapes=scratch,
        compiler_params=pltpu.CompilerParams(
            dimension_semantics=("arbitrary",),
            vmem_limit_bytes=VMEM_LIMIT_BYTES),
    )(x.reshape(batch * seq, d), *[p[0] for p in params], *c_ops)
    return out.reshape(batch, seq, d), converted


def kernel(x, w_in_even, gmlp_ws, gmlp_bs, gmlp_ln_g, gmlp_ln_b, sconv_w, w_out_even, pool_w, pool_scale, ffn_w_up, ffn_b_up, ffn_conv_w, ffn_conv_b, ffn_w_down, ln_mix_g, ln_mix_b, ln_ffn_g, ln_ffn_b):
    assert x.shape[1] % SEQ_TILE == 0 and SEQ_TILE % GMLP_BLOCK == 0
    assert x.shape[1] % POOL_TILE == 0
    assert x.shape[0] * x.shape[1] // SEQ_TILE >= CONVERT_BLOCKS
    w = dict(
        gmlp_ws=gmlp_ws,
        gmlp_bs=jnp.broadcast_to(gmlp_bs[..., None], gmlp_bs.shape + (A_HEAD_DIM,)),
        gmlp_ln_g=_rows(gmlp_ln_g), gmlp_ln_b=_rows(gmlp_ln_b), sconv_w=sconv_w,
        pool_w=pool_w.astype(BF16), pool_scale=_rows(pool_scale),
        ffn_b_up=_rows(ffn_b_up), ffn_conv_w=ffn_conv_w, ffn_conv_b=_rows(ffn_conv_b),
        ln_mix_g=_rows(ln_mix_g), ln_mix_b=_rows(ln_mix_b),
        ln_ffn_g=_rows(ln_ffn_g), ln_ffn_b=_rows(ln_ffn_b),
    )
    ffn_jobs = lambda layer: [(ffn_w_up, layer), (ffn_w_down, layer)]

    def use_ffn(converted):
        w["ffn_w_up"], w["ffn_w_down"] = converted[0][None], converted[1][None]

    w["w_in"], w["w_out"] = w_in_even[:1].astype(BF16), w_out_even[:1].astype(BF16)
    for layer in range(0, DEPTH, 2):
        x, converted = _mixer_even(x, layer // 2, layer, w, ffn_jobs(layer))
        use_ffn(converted)
        x, converted = _ffn(x, layer, w, ffn_jobs(layer + 1))
        use_ffn(converted)
        x = _mixer_odd(x, layer // 2, layer + 1, w)
        more = layer + 2 < DEPTH
        jobs = [(w_in_even, layer // 2 + 1), (w_out_even, layer // 2 + 1)] if more else []
        x, converted = _ffn(x, layer + 1, w, jobs)
        if more:
            w["w_in"], w["w_out"] = converted[0][None], converted[1][None]
    return x
```

```python
from functools import partial

import jax
import jax.numpy as jnp
from jax import lax
from jax.experimental import pallas as pl
from jax.experimental.pallas import tpu as pltpu

D_MODEL = 1024
DEPTH = 4
CHUNK = 64
GMLP_BLOCK = 128
A_HEADS = 4
A_HEAD_DIM = 128
D_A = A_HEADS * A_HEAD_DIM
D_B = 512
B_CONV = 3
C_WINDOWS = (2, 4, 8, 16)
C_GROUPS = len(C_WINDOWS)
D_C = D_MODEL // C_GROUPS
D_FF = 2816
FFN_CONV = 3
ALPHA = (2.0 * DEPTH) ** 0.25
LN_EPS = 1e-5

LANES = 128
SUBLANES = 8
BF16_ROWS = 16
SEQ_TILE = 512
FF_CHUNK = 256
N_FF_CHUNKS = D_FF // FF_CHUNK
DOWN_ROWS = 256
DOWN_PLAN = (3, 8)
OUT_ROWS = 256
MIX_TILE = 1024
POOL_TILE = 1024
POOL_QUAD_ROWS = 16
POOL_HIST = 24
VMEM_LIMIT_BYTES = 56 * 1024 * 1024

BF16 = jnp.bfloat16
F32 = jnp.float32


def _layer_norm(z, g, b):
    mu = jnp.mean(z, axis=-1, keepdims=True)
    zc = z - mu
    var = jnp.mean(zc * zc, axis=-1, keepdims=True)
    return zc * lax.rsqrt(var + LN_EPS) * g + b


def _dot(a, b):
    return jnp.dot(a, b, preferred_element_type=F32)


def _convert_blocks(src_refs, dst_refs):
    for src, dst in zip(src_refs, dst_refs, strict=True):
        dst[...] = src[...].astype(BF16)


def _causal_conv3(buf_ref, hist_ref, h, w):
    t, c = h.shape
    out = []
    for s in range(c // LANES):
        lanes = slice(s * LANES, (s + 1) * LANES)
        hs = h[:, lanes]
        buf_ref[s, 0:SUBLANES, :] = hist_ref[s]
        buf_ref[s, SUBLANES:SUBLANES + t, :] = hs
        hist_ref[s] = hs[t - SUBLANES:, :]
        h1 = buf_ref[s, SUBLANES - 1:SUBLANES - 1 + t, :]
        h2 = buf_ref[s, SUBLANES - 2:SUBLANES - 2 + t, :]
        out.append(w[0:1, lanes] * h2 + w[1:2, lanes] * h1 + w[2:3, lanes] * hs)
    return jnp.concatenate(out, axis=-1)


def _mixer_even_kernel(n_convert, x_ref, win_ref, ws_ref, bsb_ref, lng_ref, lnb_ref, cw_ref,
                       wout_ref, g_ref, b_ref, *refs):
    o_ref, buf_ref, hist_ref = refs[n_convert], refs[-2], refs[-1]

    @pl.when(pl.program_id(1) == 0)
    def _():
        hist_ref[...] = jnp.zeros_like(hist_ref)

    x = x_ref[...]
    t = x.shape[0]
    xb = x.astype(BF16)

    uv = _dot(xb, win_ref[:, :2 * D_A])
    bch = _dot(xb, win_ref[:, 2 * D_A:])

    uv = jax.nn.gelu(uv)
    row = lax.broadcasted_iota(jnp.int32, (GMLP_BLOCK, GMLP_BLOCK), 0) // CHUNK
    col = lax.broadcasted_iota(jnp.int32, (GMLP_BLOCK, GMLP_BLOCK), 1) // CHUNK
    mask = col <= row
    ya = []
    for h in range(A_HEADS):
        lanes = slice(h * A_HEAD_DIM, (h + 1) * A_HEAD_DIM)
        u = uv[:, lanes]
        v = uv[:, D_A + h * A_HEAD_DIM:D_A + (h + 1) * A_HEAD_DIM]
        vn = _layer_norm(v, lng_ref[:, lanes], lnb_ref[:, lanes]).astype(BF16)
        wm = jnp.where(mask, ws_ref[h], 0.0).astype(BF16)
        s = [_dot(wm, vn[n * GMLP_BLOCK:(n + 1) * GMLP_BLOCK, :]) + bsb_ref[h]
             for n in range(t // GMLP_BLOCK)]
        ya.append(u * jnp.concatenate(s, axis=0))

    gb = bch[:, :D_B]
    ch = bch[:, D_B:2 * D_B] * bch[:, 2 * D_B:]
    yb = gb * _causal_conv3(buf_ref, hist_ref, ch, cw_ref[...])

    ycat = jnp.concatenate(ya + [yb], axis=-1).astype(BF16)
    for r in range(0, t, OUT_ROWS):
        rows = slice(r, r + OUT_ROWS)
        y = _dot(ycat[rows], wout_ref[...])
        o_ref[rows, :] = _layer_norm(ALPHA * x[rows] + y, g_ref[...], b_ref[...])
    _convert_blocks(refs[:n_convert], refs[n_convert + 1:-2])


def _mixer_odd_kernel(x_ref, pw_ref, scale_ref, g_ref, b_ref, o_ref, buf_ref, hist_ref, sum_ref):
    j = pl.program_id(1)

    @pl.when(j == 0)
    def _():
        hist_ref[...] = jnp.zeros_like(hist_ref)

    x = x_ref[...]
    t = x.shape[0]
    for s in range(D_MODEL // LANES):
        lanes = slice(s * LANES, (s + 1) * LANES)
        buf_ref[s, 0:POOL_HIST, :] = hist_ref[s]
        buf_ref[s, POOL_HIST:POOL_HIST + t, :] = x[:, lanes]
        hist_ref[s] = x[t - POOL_HIST:, lanes]

    def trailing(s, back, rows, first=0):
        return buf_ref[s, POOL_HIST + first - back:POOL_HIST + first - back + rows, :]

    pos = j * t + lax.broadcasted_iota(jnp.int32, (t, 1), 0) + 1
    ys = []
    for gi, win in enumerate(C_WINDOWS):
        cnt = jnp.minimum(pos, win).astype(F32)
        p = []
        for s in range(gi * D_C // LANES, (gi + 1) * D_C // LANES):
            xs = x[:, s * LANES:(s + 1) * LANES]
            if win <= 4:
                acc = xs
                for k in range(1, win):
                    acc = acc + trailing(s, k, t)
            else:
                n = POOL_QUAD_ROWS + t
                quad = trailing(s, 0, n, -POOL_QUAD_ROWS)
                for k in range(1, 4):
                    quad = quad + trailing(s, k, n, -POOL_QUAD_ROWS)
                sum_ref[...] = quad
                acc = sum_ref[POOL_QUAD_ROWS:POOL_QUAD_ROWS + t, :]
                for k in range(4, win, 4):
                    acc = acc + sum_ref[POOL_QUAD_ROWS - k:POOL_QUAD_ROWS - k + t, :]
            p.append(acc / cnt - xs)
        ys.append(_dot(jnp.concatenate(p, axis=-1).astype(BF16), pw_ref[gi]))
    y = jnp.concatenate(ys, axis=-1) * scale_ref[...]
    o_ref[...] = _layer_norm(ALPHA * x + y, g_ref[...], b_ref[...])


def _gelu_tanh(x):
    k = (2.0 / jnp.pi) ** 0.5
    return x * (0.5 + 0.5 * jnp.tanh(x * (k + (0.044715 * k) * (x * x))))


def _ffn_kernel(tiles_per_seq, n_convert, x_ref, wup_ref, bup_ref, cw_ref, cb_ref, wdn_ref, g_ref,
                b_ref, *refs):
    o_ref = refs[n_convert]
    buf_ref, hist_ref, act_ref, xprev_ref = refs[2 * n_convert + 1:]
    t = x_ref.shape[0]
    slabs = FF_CHUNK // LANES
    i = pl.program_id(0)
    cur = i % 2

    @pl.when(i == 0)
    def _():
        act_ref[...] = jnp.zeros_like(act_ref)
        xprev_ref[...] = jnp.zeros_like(xprev_ref)

    @pl.when(i % tiles_per_seq == 0)
    def _():
        for slab in range(2 * D_FF // LANES):
            lanes = slice(slab * LANES, (slab + 1) * LANES)
            hist_ref[slab] = jnp.broadcast_to(-bup_ref[:, lanes], (SUBLANES, LANES))

    def step(project_up):
        x = x_ref[...]
        xb = x.astype(BF16)

        def chunk_cols(c):
            return (slice(c * FF_CHUNK, (c + 1) * FF_CHUNK),
                    slice(D_FF + c * FF_CHUNK, D_FF + (c + 1) * FF_CHUNK))

        def up_project(c, k):
            cols = chunk_cols(c)[k]
            h = _dot(xb, wup_ref[:, cols])
            for s in range(slabs):
                lanes = slice(s * LANES, (s + 1) * LANES)
                slab = cols.start // LANES + s
                stage = buf_ref.at[c % 2, k, s]
                stage[0:SUBLANES, :] = hist_ref[slab]
                stage[SUBLANES:SUBLANES + t, :] = h[:, lanes]
                hist_ref[slab] = h[t - SUBLANES:, lanes]

        def conv_and_gate(c, s):
            ys = []
            for k, cols in enumerate(chunk_cols(c)):
                lanes = slice(cols.start + s * LANES, cols.start + (s + 1) * LANES)
                w = cw_ref[:, lanes]
                const = cb_ref[:, lanes] + bup_ref[:, lanes] * (w[0:1, :] + w[1:2, :] + w[2:3, :])
                stage = buf_ref.at[c % 2, k, s]
                h0 = stage[SUBLANES:SUBLANES + t, :]
                h1 = stage[SUBLANES - 1:SUBLANES - 1 + t, :]
                h2 = stage[SUBLANES - 2:SUBLANES - 2 + t, :]
                ys.append(w[0:1, :] * h2 + w[1:2, :] * h1 + w[2:3, :] * h0 + const)
            lanes = slice(c * FF_CHUNK + s * LANES, c * FF_CHUNK + (s + 1) * LANES)
            act_ref[cur, :, lanes] = (_gelu_tanh(ys[0]) * ys[1]).astype(BF16)

        def finish_previous_rows(r):
            rows = slice(r, r + DOWN_ROWS)
            f = _dot(act_ref[1 - cur, rows, :], wdn_ref[...])
            o_ref[rows, :] = _layer_norm(ALPHA * xprev_ref[rows, :] + f, g_ref[...], b_ref[...])

        plan = dict(zip(DOWN_PLAN, range(0, t, DOWN_ROWS), strict=True))
        if not project_up:
            for r in plan.values():
                finish_previous_rows(r)
            return
        up_project(0, 0)
        up_project(0, 1)
        for c in range(N_FF_CHUNKS):
            more = c + 1 < N_FF_CHUNKS
            if more:
                up_project(c + 1, 0)
            conv_and_gate(c, 0)
            if more:
                up_project(c + 1, 1)
            if c in plan:
                finish_previous_rows(plan[c])
            conv_and_gate(c, 1)
        xprev_ref[...] = x
        _convert_blocks(refs[:n_convert], refs[n_convert + 1:2 * n_convert + 1])

    pl.when(i < pl.num_programs(0) - 1)(partial(step, True))
    pl.when(i == pl.num_programs(0) - 1)(partial(step, False))


def _layer_spec(stack, layer, single_buffer):
    shape = (None,) + stack.shape[1:]
    index = (layer,) + (0,) * (stack.ndim - 1)
    if single_buffer:
        return pl.BlockSpec(shape, lambda *_: index, pipeline_mode=pl.Buffered(1))
    return pl.BlockSpec(shape, lambda *_: index)


def _rows(v):
    return v.reshape(v.shape[0], 1, v.shape[1])


def _convert_specs(jobs, step_of, n_steps):
    in_specs, out_specs, out_shapes, operands = [], [], [], []
    for stack, layer in jobs:
        _, r, c = stack.shape
        n_blocks = max(n for n in range(1, n_steps + 1) if r % n == 0 and (r // n) % BF16_ROWS == 0)
        rows = r // n_blocks
        block = lambda *idx, n=n_blocks: jnp.minimum(step_of(*idx), n - 1)
        in_specs.append(pl.BlockSpec((None, rows, c), lambda *idx, l=layer, b=block: (l, b(*idx), 0)))
        out_specs.append(pl.BlockSpec((rows, c), lambda *idx, b=block: (b(*idx), 0)))
        out_shapes.append(jax.ShapeDtypeStruct((r, c), BF16))
        operands.append(stack)
    return in_specs, out_specs, out_shapes, operands


def _mixer_even(x, i, layer, w, convert):
    params = [
        (w["w_in"], 0, True),
        (w["gmlp_ws"], i, False),
        (w["gmlp_bs"], i, False),
        (w["gmlp_ln_g"], i, False), (w["gmlp_ln_b"], i, False),
        (w["sconv_w"], i, False),
        (w["w_out"], 0, True),
        (w["ln_mix_g"], layer, False), (w["ln_mix_b"], layer, False),
    ]
    scratch = [pltpu.VMEM((D_B // LANES, SUBLANES + MIX_TILE, LANES), F32),
               pltpu.VMEM((D_B // LANES, SUBLANES, LANES), F32)]
    batch, seq, d = x.shape
    tiles_per_seq = seq // MIX_TILE
    x_spec = pl.BlockSpec((None, MIX_TILE, d), lambda b, j: (b, j, 0))
    c_in, c_out, c_shapes, c_ops = _convert_specs(
        convert, lambda b, j: b * tiles_per_seq + j, batch * tiles_per_seq)
    out, *converted = pl.pallas_call(
        partial(_mixer_even_kernel, len(convert)),
        name="mixer_even",
        grid=(batch, tiles_per_seq),
        in_specs=[x_spec] + [_layer_spec(*p) for p in params] + c_in,
        out_specs=[x_spec] + c_out,
        out_shape=[jax.ShapeDtypeStruct(x.shape, x.dtype)] + c_shapes,
        scratch_shapes=scratch,
        compiler_params=pltpu.CompilerParams(
            dimension_semantics=("arbitrary", "arbitrary"),
            vmem_limit_bytes=VMEM_LIMIT_BYTES),
    )(x, *[p[0] for p in params], *c_ops)
    return out, converted


def _mixer_odd(x, i, layer, w):
    params = [(w["pool_w"], i, False), (w["pool_scale"], i, False),
              (w["ln_mix_g"], layer, False), (w["ln_mix_b"], layer, False)]
    batch, seq, d = x.shape
    x_spec = pl.BlockSpec((None, POOL_TILE, d), lambda b, j: (b, j, 0))
    return pl.pallas_call(
        _mixer_odd_kernel,
        name="mixer_odd",
        grid=(batch, seq // POOL_TILE),
        in_specs=[x_spec] + [_layer_spec(*p) for p in params],
        out_specs=x_spec,
        out_shape=jax.ShapeDtypeStruct(x.shape, x.dtype),
        scratch_shapes=[pltpu.VMEM((D_MODEL // LANES, POOL_HIST + POOL_TILE, LANES), F32),
                        pltpu.VMEM((D_MODEL // LANES, POOL_HIST, LANES), F32),
                        pltpu.VMEM((POOL_QUAD_ROWS + POOL_TILE, LANES), F32)],
        compiler_params=pltpu.CompilerParams(
            dimension_semantics=("arbitrary", "arbitrary"),
            vmem_limit_bytes=VMEM_LIMIT_BYTES),
    )(x, *[p[0] for p in params])


def _ffn(x, layer, w, convert):
    params = [
        (w["ffn_w_up"], 0, True),
        (w["ffn_b_up"], layer, False),
        (w["ffn_conv_w"], layer, False),
        (w["ffn_conv_b"], layer, False),
        (w["ffn_w_down"], 0, True),
        (w["ln_ffn_g"], layer, False), (w["ln_ffn_b"], layer, False),
    ]
    slabs = FF_CHUNK // LANES
    scratch = [pltpu.VMEM((2, 2, slabs, SUBLANES + SEQ_TILE, LANES), F32),
               pltpu.VMEM((2 * D_FF // LANES, SUBLANES, LANES), F32),
               pltpu.VMEM((2, SEQ_TILE, D_FF), BF16),
               pltpu.VMEM((SEQ_TILE, D_MODEL), F32)]
    batch, seq, d = x.shape
    n_tiles = batch * seq // SEQ_TILE
    c_in, c_out, c_shapes, c_ops = _convert_specs(convert, lambda i: i, n_tiles)
    out, *converted = pl.pallas_call(
        partial(_ffn_kernel, seq // SEQ_TILE, len(convert)),
        name="conv_ffn",
        grid=(n_tiles + 1,),
        in_specs=[pl.BlockSpec((SEQ_TILE, d), lambda i: (jnp.minimum(i, n_tiles - 1), 0))]
        + [_layer_spec(*p) for p in params] + c_in,
        out_specs=[pl.BlockSpec((SEQ_TILE, d), lambda i: (jnp.maximum(i - 1, 0), 0))] + c_out,
        out_shape=[jax.ShapeDtypeStruct((batch * seq, d), x.dtype)] + c_shapes,
        scratch_shapes=scratch,
        compiler_params=pltpu.CompilerParams(
            dimension_semantics=("arbitrary",),
            vmem_limit_bytes=VMEM_LIMIT_BYTES),
    )(x.reshape(batch * seq, d), *[p[0] for p in params], *c_ops)
    return out.reshape(batch, seq, d), converted


def kernel(x, w_in_even, gmlp_ws, gmlp_bs, gmlp_ln_g, gmlp_ln_b, sconv_w, w_out_even, pool_w, pool_scale, ffn_w_up, ffn_b_up, ffn_conv_w, ffn_conv_b, ffn_w_down, ln_mix_g, ln_mix_b, ln_ffn_g, ln_ffn_b):
    assert x.shape[1] % SEQ_TILE == 0 and x.shape[1] % POOL_TILE == 0
    assert x.shape[1] % MIX_TILE == 0 and MIX_TILE % GMLP_BLOCK == 0
    w = dict(
        gmlp_ws=gmlp_ws,
        gmlp_bs=jnp.broadcast_to(gmlp_bs[..., None], gmlp_bs.shape + (A_HEAD_DIM,)),
        gmlp_ln_g=_rows(gmlp_ln_g), gmlp_ln_b=_rows(gmlp_ln_b), sconv_w=sconv_w,
        pool_w=pool_w.astype(BF16), pool_scale=_rows(pool_scale),
        ffn_b_up=_rows(ffn_b_up), ffn_conv_w=ffn_conv_w, ffn_conv_b=_rows(ffn_conv_b),
        ln_mix_g=_rows(ln_mix_g), ln_mix_b=_rows(ln_mix_b),
        ln_ffn_g=_rows(ln_ffn_g), ln_ffn_b=_rows(ln_ffn_b),
    )
    ffn_jobs = lambda layer: [(ffn_w_up, layer), (ffn_w_down, layer)]

    def use_ffn(converted):
        w["ffn_w_up"], w["ffn_w_down"] = converted[0][None], converted[1][None]

    w["w_in"], w["w_out"] = w_in_even[:1].astype(BF16), w_out_even[:1].astype(BF16)
    for layer in range(0, DEPTH, 2):
        x, converted = _mixer_even(x, layer // 2, layer, w, ffn_jobs(layer))
        use_ffn(converted)
        x, converted = _ffn(x, layer, w, ffn_jobs(layer + 1))
        use_ffn(converted)
        x = _mixer_odd(x, layer // 2, layer + 1, w)
        more = layer + 2 < DEPTH
        jobs = [(w_in_even, layer // 2 + 1), (w_out_even, layer // 2 + 1)] if more else []
        x, converted = _ffn(x, layer + 1, w, jobs)
        if more:
            w["w_in"], w["w_out"] = converted[0][None], converted[1][None]
    return x
```

```python
from functools import partial

import jax
import jax.numpy as jnp
from jax import lax
from jax.experimental import pallas as pl
from jax.experimental.pallas import tpu as pltpu

D_MODEL = 1024
DEPTH = 4
CHUNK = 64
GMLP_BLOCK = 128
A_HEADS = 4
A_HEAD_DIM = 128
D_A = A_HEADS * A_HEAD_DIM
D_B = 512
B_CONV = 3
C_WINDOWS = (2, 4, 8, 16)
C_GROUPS = len(C_WINDOWS)
D_C = D_MODEL // C_GROUPS
D_FF = 2816
FFN_CONV = 3
ALPHA = (2.0 * DEPTH) ** 0.25
LN_EPS = 1e-5

LANES = 128
SUBLANES = 8
BF16_ROWS = 16
SEQ_TILE = 512
FF_CHUNK = 256
N_FF_CHUNKS = D_FF // FF_CHUNK
DOWN_ROWS = 256
DOWN_PLAN = (3, 8)
CONVERT_BLOCKS = 32
OUT_ROWS = 256
POOL_TILE = 1024
POOL_QUAD_ROWS = 16
POOL_HIST = 24
VMEM_LIMIT_BYTES = 56 * 1024 * 1024

BF16 = jnp.bfloat16
F32 = jnp.float32


def _layer_norm(z, g, b):
    mu = jnp.mean(z, axis=-1, keepdims=True)
    zc = z - mu
    var = jnp.mean(zc * zc, axis=-1, keepdims=True)
    return zc * lax.rsqrt(var + LN_EPS) * g + b


def _dot(a, b):
    return jnp.dot(a, b, preferred_element_type=F32)


def _convert_blocks(src_refs, dst_refs):
    for src, dst in zip(src_refs, dst_refs, strict=True):
        dst[...] = src[...].astype(BF16)


def _causal_conv3(buf_ref, hist_ref, h, w):
    t, c = h.shape
    out = []
    for s in range(c // LANES):
        lanes = slice(s * LANES, (s + 1) * LANES)
        hs = h[:, lanes]
        buf_ref[s, 0:SUBLANES, :] = hist_ref[s]
        buf_ref[s, SUBLANES:SUBLANES + t, :] = hs
        hist_ref[s] = hs[t - SUBLANES:, :]
        h1 = buf_ref[s, SUBLANES - 1:SUBLANES - 1 + t, :]
        h2 = buf_ref[s, SUBLANES - 2:SUBLANES - 2 + t, :]
        out.append(w[0:1, lanes] * h2 + w[1:2, lanes] * h1 + w[2:3, lanes] * hs)
    return jnp.concatenate(out, axis=-1)


def _mixer_even_kernel(n_convert, x_ref, win_ref, ws_ref, bsb_ref, lng_ref, lnb_ref, cw_ref,
                       wout_ref, g_ref, b_ref, *refs):
    o_ref, buf_ref, hist_ref = refs[n_convert], refs[-2], refs[-1]

    @pl.when(pl.program_id(1) == 0)
    def _():
        hist_ref[...] = jnp.zeros_like(hist_ref)

    x = x_ref[...]
    t = x.shape[0]
    xb = x.astype(BF16)

    uv = _dot(xb, win_ref[:, :2 * D_A])
    bch = _dot(xb, win_ref[:, 2 * D_A:])

    uv = jax.nn.gelu(uv)
    row = lax.broadcasted_iota(jnp.int32, (GMLP_BLOCK, GMLP_BLOCK), 0) // CHUNK
    col = lax.broadcasted_iota(jnp.int32, (GMLP_BLOCK, GMLP_BLOCK), 1) // CHUNK
    mask = col <= row
    ya = []
    for h in range(A_HEADS):
        lanes = slice(h * A_HEAD_DIM, (h + 1) * A_HEAD_DIM)
        u = uv[:, lanes]
        v = uv[:, D_A + h * A_HEAD_DIM:D_A + (h + 1) * A_HEAD_DIM]
        vn = _layer_norm(v, lng_ref[:, lanes], lnb_ref[:, lanes]).astype(BF16)
        wm = jnp.where(mask, ws_ref[h], 0.0).astype(BF16)
        s = [_dot(wm, vn[n * GMLP_BLOCK:(n + 1) * GMLP_BLOCK, :]) + bsb_ref[h]
             for n in range(t // GMLP_BLOCK)]
        ya.append(u * jnp.concatenate(s, axis=0))

    gb = bch[:, :D_B]
    ch = bch[:, D_B:2 * D_B] * bch[:, 2 * D_B:]
    yb = gb * _causal_conv3(buf_ref, hist_ref, ch, cw_ref[...])

    ycat = jnp.concatenate(ya + [yb], axis=-1).astype(BF16)
    for r in range(0, t, OUT_ROWS):
        rows = slice(r, r + OUT_ROWS)
        y = _dot(ycat[rows], wout_ref[...])
        o_ref[rows, :] = _layer_norm(ALPHA * x[rows] + y, g_ref[...], b_ref[...])
    _convert_blocks(refs[:n_convert], refs[n_convert + 1:-2])


def _mixer_odd_kernel(x_ref, pw_ref, scale_ref, g_ref, b_ref, o_ref, buf_ref, hist_ref, sum_ref):
    j = pl.program_id(1)

    @pl.when(j == 0)
    def _():
        hist_ref[...] = jnp.zeros_like(hist_ref)

    x = x_ref[...]
    t = x.shape[0]
    for s in range(D_MODEL // LANES):
        lanes = slice(s * LANES, (s + 1) * LANES)
        buf_ref[s, 0:POOL_HIST, :] = hist_ref[s]
        buf_ref[s, POOL_HIST:POOL_HIST + t, :] = x[:, lanes]
        hist_ref[s] = x[t - POOL_HIST:, lanes]

    def trailing(s, back, rows, first=0):
        return buf_ref[s, POOL_HIST + first - back:POOL_HIST + first - back + rows, :]

    pos = j * t + lax.broadcasted_iota(jnp.int32, (t, 1), 0) + 1
    ys = []
    for gi, win in enumerate(C_WINDOWS):
        cnt = jnp.minimum(pos, win).astype(F32)
        p = []
        for s in range(gi * D_C // LANES, (gi + 1) * D_C // LANES):
            xs = x[:, s * LANES:(s + 1) * LANES]
            if win <= 4:
                acc = xs
                for k in range(1, win):
                    acc = acc + trailing(s, k, t)
            else:
                n = POOL_QUAD_ROWS + t
                quad = trailing(s, 0, n, -POOL_QUAD_ROWS)
                for k in range(1, 4):
                    quad = quad + trailing(s, k, n, -POOL_QUAD_ROWS)
                sum_ref[...] = quad
                acc = sum_ref[POOL_QUAD_ROWS:POOL_QUAD_ROWS + t, :]
                for k in range(4, win, 4):
                    acc = acc + sum_ref[POOL_QUAD_ROWS - k:POOL_QUAD_ROWS - k + t, :]
            p.append(acc / cnt - xs)
        ys.append(_dot(jnp.concatenate(p, axis=-1).astype(BF16), pw_ref[gi]))
    y = jnp.concatenate(ys, axis=-1) * scale_ref[...]
    o_ref[...] = _layer_norm(ALPHA * x + y, g_ref[...], b_ref[...])


def _gelu_tanh(x):
    k = (2.0 / jnp.pi) ** 0.5
    return x * (0.5 + 0.5 * jnp.tanh(x * (k + (0.044715 * k) * (x * x))))


def _ffn_kernel(tiles_per_seq, n_convert, x_ref, wup_ref, bup_ref, cw_ref, cb_ref, wdn_ref, g_ref,
                b_ref, *refs):
    o_ref = refs[n_convert]
    buf_ref, hist_ref, act_ref, xprev_ref = refs[2 * n_convert + 1:]
    t = x_ref.shape[0]
    slabs = FF_CHUNK // LANES
    i = pl.program_id(0)
    cur = i % 2

    @pl.when(i % tiles_per_seq == 0)
    def _():
        for slab in range(2 * D_FF // LANES):
            lanes = slice(slab * LANES, (slab + 1) * LANES)
            hist_ref[slab] = jnp.broadcast_to(-bup_ref[:, lanes], (SUBLANES, LANES))

    def step(project_up, finish_previous):
        x = x_ref[...]
        xb = x.astype(BF16)

        def chunk_cols(c):
            return (slice(c * FF_CHUNK, (c + 1) * FF_CHUNK),
                    slice(D_FF + c * FF_CHUNK, D_FF + (c + 1) * FF_CHUNK))

        def up_project(c, k):
            cols = chunk_cols(c)[k]
            h = _dot(xb, wup_ref[:, cols])
            for s in range(slabs):
                lanes = slice(s * LANES, (s + 1) * LANES)
                slab = cols.start // LANES + s
                stage = buf_ref.at[c % 2, k, s]
                stage[0:SUBLANES, :] = hist_ref[slab]
                stage[SUBLANES:SUBLANES + t, :] = h[:, lanes]
                hist_ref[slab] = h[t - SUBLANES:, lanes]

        def conv_and_gate(c, s):
            ys = []
            for k, cols in enumerate(chunk_cols(c)):
                lanes = slice(cols.start + s * LANES, cols.start + (s + 1) * LANES)
                w = cw_ref[:, lanes]
                const = cb_ref[:, lanes] + bup_ref[:, lanes] * (w[0:1, :] + w[1:2, :] + w[2:3, :])
                stage = buf_ref.at[c % 2, k, s]
                h0 = stage[SUBLANES:SUBLANES + t, :]
                h1 = stage[SUBLANES - 1:SUBLANES - 1 + t, :]
                h2 = stage[SUBLANES - 2:SUBLANES - 2 + t, :]
                ys.append(w[0:1, :] * h2 + w[1:2, :] * h1 + w[2:3, :] * h0 + const)
            lanes = slice(c * FF_CHUNK + s * LANES, c * FF_CHUNK + (s + 1) * LANES)
            act_ref[cur, :, lanes] = (_gelu_tanh(ys[0]) * ys[1]).astype(BF16)

        def finish_previous_rows(r):
            rows = slice(r, r + DOWN_ROWS)
            f = _dot(act_ref[1 - cur, rows, :], wdn_ref[...])
            o_ref[rows, :] = _layer_norm(ALPHA * xprev_ref[rows, :] + f, g_ref[...], b_ref[...])

        plan = dict(zip(DOWN_PLAN, range(0, t, DOWN_ROWS), strict=True)) if finish_previous else {}
        if not project_up:
            for r in plan.values():
                finish_previous_rows(r)
            return
        up_project(0, 0)
        up_project(0, 1)
        for c in range(N_FF_CHUNKS):
            more = c + 1 < N_FF_CHUNKS
            if more:
                up_project(c + 1, 0)
            conv_and_gate(c, 0)
            if more:
                up_project(c + 1, 1)
            if c in plan:
                finish_previous_rows(plan[c])
            conv_and_gate(c, 1)
        xprev_ref[...] = x
        _convert_blocks(refs[:n_convert], refs[n_convert + 1:2 * n_convert + 1])

    last_step = pl.num_programs(0) - 1
    pl.when(i == 0)(partial(step, True, False))
    pl.when(jnp.logical_and(i > 0, i < last_step))(partial(step, True, True))
    pl.when(i == last_step)(partial(step, False, True))


def _layer_spec(stack, layer, single_buffer):
    shape = (None,) + stack.shape[1:]
    index = (layer,) + (0,) * (stack.ndim - 1)
    if single_buffer:
        return pl.BlockSpec(shape, lambda *_: index, pipeline_mode=pl.Buffered(1))
    return pl.BlockSpec(shape, lambda *_: index)


def _rows(v):
    return v.reshape(v.shape[0], 1, v.shape[1])


def _convert_specs(jobs, step_of):
    in_specs, out_specs, out_shapes, operands = [], [], [], []
    for stack, layer in jobs:
        _, r, c = stack.shape
        n_blocks = max(n for n in range(1, CONVERT_BLOCKS + 1)
                       if r % n == 0 and (r // n) % BF16_ROWS == 0)
        rows = r // n_blocks
        block = lambda *idx, n=n_blocks: jnp.minimum(step_of(*idx), n - 1)
        in_specs.append(pl.BlockSpec((None, rows, c), lambda *idx, l=layer, b=block: (l, b(*idx), 0)))
        out_specs.append(pl.BlockSpec((rows, c), lambda *idx, b=block: (b(*idx), 0)))
        out_shapes.append(jax.ShapeDtypeStruct((r, c), BF16))
        operands.append(stack)
    return in_specs, out_specs, out_shapes, operands


def _mixer_even(x, i, layer, w, convert):
    params = [
        (w["w_in"], 0, True),
        (w["gmlp_ws"], i, False),
        (w["gmlp_bs"], i, False),
        (w["gmlp_ln_g"], i, False), (w["gmlp_ln_b"], i, False),
        (w["sconv_w"], i, False),
        (w["w_out"], 0, True),
        (w["ln_mix_g"], layer, False), (w["ln_mix_b"], layer, False),
    ]
    scratch = [pltpu.VMEM((D_B // LANES, SUBLANES + SEQ_TILE, LANES), F32),
               pltpu.VMEM((D_B // LANES, SUBLANES, LANES), F32)]
    batch, seq, d = x.shape
    tiles_per_seq = seq // SEQ_TILE
    x_spec = pl.BlockSpec((None, SEQ_TILE, d), lambda b, j: (b, j, 0))
    c_in, c_out, c_shapes, c_ops = _convert_specs(convert, lambda b, j: b * tiles_per_seq + j)
    out, *converted = pl.pallas_call(
        partial(_mixer_even_kernel, len(convert)),
        name="mixer_even",
        grid=(batch, tiles_per_seq),
        in_specs=[x_spec] + [_layer_spec(*p) for p in params] + c_in,
        out_specs=[x_spec] + c_out,
        out_shape=[jax.ShapeDtypeStruct(x.shape, x.dtype)] + c_shapes,
        scratch_shapes=scratch,
        compiler_params=pltpu.CompilerParams(
            dimension_semantics=("arbitrary", "arbitrary"),
            vmem_limit_bytes=VMEM_LIMIT_BYTES),
    )(x, *[p[0] for p in params], *c_ops)
    return out, converted


def _mixer_odd(x, i, layer, w):
    params = [(w["pool_w"], i, False), (w["pool_scale"], i, False),
              (w["ln_mix_g"], layer, False), (w["ln_mix_b"], layer, False)]
    batch, seq, d = x.shape
    x_spec = pl.BlockSpec((None, POOL_TILE, d), lambda b, j: (b, j, 0))
    return pl.pallas_call(
        _mixer_odd_kernel,
        name="mixer_odd",
        grid=(batch, seq // POOL_TILE),
        in_specs=[x_spec] + [_layer_spec(*p) for p in params],
        out_specs=x_spec,
        out_shape=jax.ShapeDtypeStruct(x.shape, x.dtype),
        scratch_shapes=[pltpu.VMEM((D_MODEL // LANES, POOL_HIST + POOL_TILE, LANES), F32),
                        pltpu.VMEM((D_MODEL // LANES, POOL_HIST, LANES), F32),
                        pltpu.VMEM((POOL_QUAD_ROWS + POOL_TILE, LANES), F32)],
        compiler_params=pltpu.CompilerParams(
            dimension_semantics=("arbitrary", "arbitrary"),
            vmem_limit_bytes=VMEM_LIMIT_BYTES),
    )(x, *[p[0] for p in params])


def _ffn(x, layer, w, convert):
    params = [
        (w["ffn_w_up"], 0, True),
        (w["ffn_b_up"], layer, False),
        (w["ffn_conv_w"], layer, False),
        (w["ffn_conv_b"], layer, False),
        (w["ffn_w_down"], 0, True),
        (w["ln_ffn_g"], layer, False), (w["ln_ffn_b"], layer, False),
    ]
    slabs = FF_CHUNK // LANES
    scratch = [pltpu.VMEM((2, 2, slabs, SUBLANES + SEQ_TILE, LANES), F32),
               pltpu.VMEM((2 * D_FF // LANES, SUBLANES, LANES), F32),
               pltpu.VMEM((2, SEQ_TILE, D_FF), BF16),
               pltpu.VMEM((SEQ_TILE, D_MODEL), F32)]
    batch, seq, d = x.shape
    n_tiles = batch * seq // SEQ_TILE
    c_in, c_out, c_shapes, c_ops = _convert_specs(convert, lambda i: i)
    out, *converted = pl.pallas_call(
        partial(_ffn_kernel, seq // SEQ_TILE, len(convert)),
        name="conv_ffn",
        grid=(n_tiles + 1,),
        in_specs=[pl.BlockSpec((SEQ_TILE, d), lambda i: (jnp.minimum(i, n_tiles - 1), 0))]
        + [_layer_spec(*p) for p in params] + c_in,
        out_specs=[pl.BlockSpec((SEQ_TILE, d), lambda i: (jnp.maximum(i - 1, 0), 0))] + c_out,
        out_shape=[jax.ShapeDtypeStruct((batch * seq, d), x.dtype)] + c_shapes,
        scratch_shapes=scratch,
        compiler_params=pltpu.CompilerParams(
            dimension_semantics=("arbitrary",),
            vmem_limit_bytes=VMEM_LIMIT_BYTES),
    )(x.reshape(batch * seq, d), *[p[0] for p in params], *c_ops)
    return out.reshape(batch, seq, d), converted


def kernel(x, w_in_even, gmlp_ws, gmlp_bs, gmlp_ln_g, gmlp_ln_b, sconv_w, w_out_even, pool_w, pool_scale, ffn_w_up, ffn_b_up, ffn_conv_w, ffn_conv_b, ffn_w_down, ln_mix_g, ln_mix_b, ln_ffn_g, ln_ffn_b):
    assert x.shape[1] % SEQ_TILE == 0 and SEQ_TILE % GMLP_BLOCK == 0
    assert x.shape[1] % POOL_TILE == 0
    assert x.shape[0] * x.shape[1] // SEQ_TILE >= CONVERT_BLOCKS
    w = dict(
        gmlp_ws=gmlp_ws,
        gmlp_bs=jnp.broadcast_to(gmlp_bs[..., None], gmlp_bs.shape + (A_HEAD_DIM,)),
        gmlp_ln_g=_rows(gmlp_ln_g), gmlp_ln_b=_rows(gmlp_ln_b), sconv_w=sconv_w,
        pool_w=pool_w.astype(BF16), pool_scale=_rows(pool_scale),
        ffn_b_up=_rows(ffn_b_up), ffn_conv_w=ffn_conv_w, ffn_conv_b=_rows(ffn_conv_b),
        ln_mix_g=_rows(ln_mix_g), ln_mix_b=_rows(ln_mix_b),
        ln_ffn_g=_rows(ln_ffn_g), ln_ffn_b=_rows(ln_ffn_b),
    )
    ffn_jobs = lambda layer: [(ffn_w_up, layer), (ffn_w_down, layer)]

    def use_ffn(converted):
        w["ffn_w_up"], w["ffn_w_down"] = converted[0][None], converted[1][None]

    w["w_in"], w["w_out"] = w_in_even[:1].astype(BF16), w_out_even[:1].astype(BF16)
    for layer in range(0, DEPTH, 2):
        x, converted = _mixer_even(x, layer // 2, layer, w, ffn_jobs(layer))
        use_ffn(converted)
        x, converted = _ffn(x, layer, w, ffn_jobs(layer + 1))
        use_ffn(converted)
        x = _mixer_odd(x, layer // 2, layer + 1, w)
        more = layer + 2 < DEPTH
        jobs = [(w_in_even, layer // 2 + 1), (w_out_even, layer // 2 + 1)] if more else []
        x, converted = _ffn(x, layer + 1, w, jobs)
        if more:
            w["w_in"], w["w_out"] = converted[0][None], converted[1][None]
    return x
```

```python
from functools import partial

import jax
import jax.numpy as jnp
from jax import lax
from jax.experimental import pallas as pl
from jax.experimental.pallas import tpu as pltpu

D_MODEL = 1024
DEPTH = 4
CHUNK = 64
GMLP_BLOCK = 128
A_HEADS = 4
A_HEAD_DIM = 128
D_A = A_HEADS * A_HEAD_DIM
D_B = 512
C_WINDOWS = (2, 4, 8, 16)
C_GROUPS = len(C_WINDOWS)
D_C = D_MODEL // C_GROUPS
D_FF = 2816
ALPHA = (2.0 * DEPTH) ** 0.25
LN_EPS = 1e-5

LANES = 128
SUBLANES = 8
BF16_ROWS = 16
SEQ_TILE = 512
FF_CHUNK = 256
N_FF_CHUNKS = D_FF // FF_CHUNK
DOWN_ROWS = 256
DOWN_PLAN = (3, 8)
CONVERT_BLOCKS = 32
OUT_ROWS = 256
POOL_TILE = 2048
POOL_QUAD_ROWS = 16
POOL_HIST = 24
VMEM_LIMIT_BYTES = 56 * 1024 * 1024

BF16 = jnp.bfloat16
F32 = jnp.float32


def _layer_norm(z, g, b):
    mu = jnp.mean(z, axis=-1, keepdims=True)
    zc = z - mu
    var = jnp.mean(zc * zc, axis=-1, keepdims=True)
    return zc * lax.rsqrt(var + LN_EPS) * g + b


def _dot(a, b):
    return jnp.dot(a, b, preferred_element_type=F32)


def _gelu_tanh(x):
    k = (2.0 / jnp.pi) ** 0.5
    return x * (0.5 + 0.5 * jnp.tanh(x * (k + (0.044715 * k) * (x * x))))


def _convert_blocks(src_refs, dst_refs):
    for src, dst in zip(src_refs, dst_refs, strict=True):
        dst[...] = src[...].astype(BF16)


def _causal_conv3(buf_ref, hist_ref, h, w):
    t, c = h.shape
    out = []
    for s in range(c // LANES):
        lanes = slice(s * LANES, (s + 1) * LANES)
        hs = h[:, lanes]
        buf_ref[s, 0:SUBLANES, :] = hist_ref[s]
        buf_ref[s, SUBLANES:SUBLANES + t, :] = hs
        hist_ref[s] = hs[t - SUBLANES:, :]
        h1 = buf_ref[s, SUBLANES - 1:SUBLANES - 1 + t, :]
        h2 = buf_ref[s, SUBLANES - 2:SUBLANES - 2 + t, :]
        out.append(w[0:1, lanes] * h2 + w[1:2, lanes] * h1 + w[2:3, lanes] * hs)
    return jnp.concatenate(out, axis=-1)


def _mixer_even_kernel(n_convert, x_ref, win_ref, ws_ref, bsb_ref, lng_ref, lnb_ref, cw_ref,
                       wout_ref, g_ref, b_ref, *refs):
    o_ref, buf_ref, hist_ref = refs[n_convert], refs[-2], refs[-1]

    @pl.when(pl.program_id(1) == 0)
    def _():
        hist_ref[...] = jnp.zeros_like(hist_ref)

    x = x_ref[...]
    t = x.shape[0]
    xb = x.astype(BF16)

    uv = _dot(xb, win_ref[:, :2 * D_A])
    bch = _dot(xb, win_ref[:, 2 * D_A:])

    uv = _gelu_tanh(uv)
    row = lax.broadcasted_iota(jnp.int32, (GMLP_BLOCK, GMLP_BLOCK), 0) // CHUNK
    col = lax.broadcasted_iota(jnp.int32, (GMLP_BLOCK, GMLP_BLOCK), 1) // CHUNK
    mask = col <= row
    ya = []
    for h in range(A_HEADS):
        lanes = slice(h * A_HEAD_DIM, (h + 1) * A_HEAD_DIM)
        u = uv[:, lanes]
        v = uv[:, D_A + h * A_HEAD_DIM:D_A + (h + 1) * A_HEAD_DIM]
        vn = _layer_norm(v, lng_ref[:, lanes], lnb_ref[:, lanes]).astype(BF16)
        wm = jnp.where(mask, ws_ref[h], 0.0).astype(BF16)
        s = [_dot(wm, vn[n * GMLP_BLOCK:(n + 1) * GMLP_BLOCK, :]) + bsb_ref[h]
             for n in range(t // GMLP_BLOCK)]
        ya.append(u * jnp.concatenate(s, axis=0))

    gb = bch[:, :D_B]
    ch = bch[:, D_B:2 * D_B] * bch[:, 2 * D_B:]
    yb = gb * _causal_conv3(buf_ref, hist_ref, ch, cw_ref[...])

    ycat = jnp.concatenate(ya + [yb], axis=-1).astype(BF16)
    for r in range(0, t, OUT_ROWS):
        rows = slice(r, r + OUT_ROWS)
        y = _dot(ycat[rows], wout_ref[...])
        o_ref[rows, :] = _layer_norm(ALPHA * x[rows] + y, g_ref[...], b_ref[...])
    _convert_blocks(refs[:n_convert], refs[n_convert + 1:-2])


def _mixer_odd_kernel(x_ref, pw_ref, scale_ref, g_ref, b_ref, o_ref, buf_ref, hist_ref, sum_ref):
    j = pl.program_id(1)

    @pl.when(j == 0)
    def _():
        hist_ref[...] = jnp.zeros_like(hist_ref)

    x = x_ref[...]
    t = x.shape[0]
    for s in range(D_MODEL // LANES):
        lanes = slice(s * LANES, (s + 1) * LANES)
        buf_ref[s, 0:POOL_HIST, :] = hist_ref[s]
        buf_ref[s, POOL_HIST:POOL_HIST + t, :] = x[:, lanes]
        hist_ref[s] = x[t - POOL_HIST:, lanes]

    def trailing(s, back, rows, first=0):
        return buf_ref[s, POOL_HIST + first - back:POOL_HIST + first - back + rows, :]

    pos = j * t + lax.broadcasted_iota(jnp.int32, (t, 1), 0) + 1
    ys = []
    for gi, win in enumerate(C_WINDOWS):
        cnt = jnp.minimum(pos, win).astype(F32)
        p = []
        for s in range(gi * D_C // LANES, (gi + 1) * D_C // LANES):
            xs = x[:, s * LANES:(s + 1) * LANES]
            if win <= 4:
                acc = xs
                for k in range(1, win):
                    acc = acc + trailing(s, k, t)
            else:
                n = POOL_QUAD_ROWS + t
                quad = trailing(s, 0, n, -POOL_QUAD_ROWS)
                for k in range(1, 4):
                    quad = quad + trailing(s, k, n, -POOL_QUAD_ROWS)
                sum_ref[...] = quad
                acc = sum_ref[POOL_QUAD_ROWS:POOL_QUAD_ROWS + t, :]
                for k in range(4, win, 4):
                    acc = acc + sum_ref[POOL_QUAD_ROWS - k:POOL_QUAD_ROWS - k + t, :]
            p.append(acc / cnt - xs)
        ys.append(_dot(jnp.concatenate(p, axis=-1).astype(BF16), pw_ref[gi]))
    y = jnp.concatenate(ys, axis=-1) * scale_ref[...]
    o_ref[...] = _layer_norm(ALPHA * x + y, g_ref[...], b_ref[...])


def _ffn_kernel(tiles_per_seq, n_convert, x_ref, wup_ref, bup_ref, cw_ref, cb_ref, wdn_ref, g_ref,
                b_ref, *refs):
    o_ref = refs[n_convert]
    buf_ref, hist_ref, act_ref, xprev_ref = refs[2 * n_convert + 1:]
    t = x_ref.shape[0]
    slabs = FF_CHUNK // LANES
    i = pl.program_id(0)
    cur = i % 2

    @pl.when(i % tiles_per_seq == 0)
    def _():
        for slab in range(2 * D_FF // LANES):
            lanes = slice(slab * LANES, (slab + 1) * LANES)
            hist_ref[slab] = jnp.broadcast_to(-bup_ref[:, lanes], (SUBLANES, LANES))

    def step(project_up, finish_previous):
        x = x_ref[...]
        xb = x.astype(BF16)

        def chunk_cols(c):
            return (slice(c * FF_CHUNK, (c + 1) * FF_CHUNK),
                    slice(D_FF + c * FF_CHUNK, D_FF + (c + 1) * FF_CHUNK))

        def up_project(c, k):
            cols = chunk_cols(c)[k]
            h = _dot(xb, wup_ref[:, cols])
            for s in range(slabs):
                lanes = slice(s * LANES, (s + 1) * LANES)
                slab = cols.start // LANES + s
                stage = buf_ref.at[c % 2, k, s]
                stage[0:SUBLANES, :] = hist_ref[slab]
                stage[SUBLANES:SUBLANES + t, :] = h[:, lanes]
                hist_ref[slab] = h[t - SUBLANES:, lanes]

        def conv_and_gate(c, s):
            ys = []
            for k, cols in enumerate(chunk_cols(c)):
                lanes = slice(cols.start + s * LANES, cols.start + (s + 1) * LANES)
                w = cw_ref[:, lanes]
                const = cb_ref[:, lanes] + bup_ref[:, lanes] * (w[0:1, :] + w[1:2, :] + w[2:3, :])
                stage = buf_ref.at[c % 2, k, s]
                h0 = stage[SUBLANES:SUBLANES + t, :]
                h1 = stage[SUBLANES - 1:SUBLANES - 1 + t, :]
                h2 = stage[SUBLANES - 2:SUBLANES - 2 + t, :]
                ys.append(w[0:1, :] * h2 + w[1:2, :] * h1 + w[2:3, :] * h0 + const)
            lanes = slice(c * FF_CHUNK + s * LANES, c * FF_CHUNK + (s + 1) * LANES)
            act_ref[cur, :, lanes] = (_gelu_tanh(ys[0]) * ys[1]).astype(BF16)

        def finish_previous_rows(r):
            rows = slice(r, r + DOWN_ROWS)
            f = _dot(act_ref[1 - cur, rows, :], wdn_ref[...])
            o_ref[rows, :] = _layer_norm(ALPHA * xprev_ref[rows, :] + f, g_ref[...], b_ref[...])

        plan = dict(zip(DOWN_PLAN, range(0, t, DOWN_ROWS), strict=True)) if finish_previous else {}
        if not project_up:
            for r in plan.values():
                finish_previous_rows(r)
            return
        up_project(0, 0)
        up_project(0, 1)
        for c in range(N_FF_CHUNKS):
            more = c + 1 < N_FF_CHUNKS
            if more:
                up_project(c + 1, 0)
            conv_and_gate(c, 0)
            if more:
                up_project(c + 1, 1)
            if c in plan:
                finish_previous_rows(plan[c])
            conv_and_gate(c, 1)
        xprev_ref[...] = x
        _convert_blocks(refs[:n_convert], refs[n_convert + 1:2 * n_convert + 1])

    last_step = pl.num_programs(0) - 1
    pl.when(i == 0)(partial(step, True, False))
    pl.when(jnp.logical_and(i > 0, i < last_step))(partial(step, True, True))
    pl.when(i == last_step)(partial(step, False, True))


def _layer_spec(stack, layer, single_buffer):
    shape = (None,) + stack.shape[1:]
    index = (layer,) + (0,) * (stack.ndim - 1)
    if single_buffer:
        return pl.BlockSpec(shape, lambda *_: index, pipeline_mode=pl.Buffered(1))
    return pl.BlockSpec(shape, lambda *_: index)


def _rows(v):
    return v.reshape(v.shape[0], 1, v.shape[1])


def _convert_specs(jobs, step_of):
    in_specs, out_specs, out_shapes, operands = [], [], [], []
    for stack, layer in jobs:
        _, r, c = stack.shape
        n_blocks = max(n for n in range(1, CONVERT_BLOCKS + 1)
                       if r % n == 0 and (r // n) % BF16_ROWS == 0)
        rows = r // n_blocks
        block = lambda *idx, n=n_blocks: jnp.minimum(step_of(*idx), n - 1)
        in_specs.append(pl.BlockSpec((None, rows, c), lambda *idx, l=layer, b=block: (l, b(*idx), 0)))
        out_specs.append(pl.BlockSpec((rows, c), lambda *idx, b=block: (b(*idx), 0)))
        out_shapes.append(jax.ShapeDtypeStruct((r, c), BF16))
        operands.append(stack)
    return in_specs, out_specs, out_shapes, operands


def _mixer_even(x, i, layer, w, convert):
    params = [
        (w["w_in"], 0, True),
        (w["gmlp_ws"], i, False),
        (w["gmlp_bs"], i, False),
        (w["gmlp_ln_g"], i, False), (w["gmlp_ln_b"], i, False),
        (w["sconv_w"], i, False),
        (w["w_out"], 0, True),
        (w["ln_mix_g"], layer, False), (w["ln_mix_b"], layer, False),
    ]
    scratch = [pltpu.VMEM((D_B // LANES, SUBLANES + SEQ_TILE, LANES), F32),
               pltpu.VMEM((D_B // LANES, SUBLANES, LANES), F32)]
    batch, seq, d = x.shape
    tiles_per_seq = seq // SEQ_TILE
    x_spec = pl.BlockSpec((None, SEQ_TILE, d), lambda b, j: (b, j, 0))
    c_in, c_out, c_shapes, c_ops = _convert_specs(convert, lambda b, j: b * tiles_per_seq + j)
    out, *converted = pl.pallas_call(
        partial(_mixer_even_kernel, len(convert)),
        name="mixer_even",
        grid=(batch, tiles_per_seq),
        in_specs=[x_spec] + [_layer_spec(*p) for p in params] + c_in,
        out_specs=[x_spec] + c_out,
        out_shape=[jax.ShapeDtypeStruct(x.shape, x.dtype)] + c_shapes,
        scratch_shapes=scratch,
        compiler_params=pltpu.CompilerParams(
            dimension_semantics=("arbitrary", "arbitrary"),
            vmem_limit_bytes=VMEM_LIMIT_BYTES),
    )(x, *[p[0] for p in params], *c_ops)
    return out, converted


def _mixer_odd(x, i, layer, w):
    params = [(w["pool_w"], i, False), (w["pool_scale"], i, False),
              (w["ln_mix_g"], layer, False), (w["ln_mix_b"], layer, False)]
    batch, seq, d = x.shape
    x_spec = pl.BlockSpec((None, POOL_TILE, d), lambda b, j: (b, j, 0))
    return pl.pallas_call(
        _mixer_odd_kernel,
        name="mixer_odd",
        grid=(batch, seq // POOL_TILE),
        in_specs=[x_spec] + [_layer_spec(*p) for p in params],
        out_specs=x_spec,
        out_shape=jax.ShapeDtypeStruct(x.shape, x.dtype),
        scratch_shapes=[pltpu.VMEM((D_MODEL // LANES, POOL_HIST + POOL_TILE, LANES), F32),
                        pltpu.VMEM((D_MODEL // LANES, POOL_HIST, LANES), F32),
                        pltpu.VMEM((POOL_QUAD_ROWS + POOL_TILE, LANES), F32)],
        compiler_params=pltpu.CompilerParams(
            dimension_semantics=("arbitrary", "arbitrary"),
            vmem_limit_bytes=VMEM_LIMIT_BYTES),
    )(x, *[p[0] for p in params])


def _ffn(x, layer, w, convert):
    params = [
        (w["ffn_w_up"], 0, True),
        (w["ffn_b_up"], layer, False),
        (w["ffn_conv_w"], layer, False),
        (w["ffn_conv_b"], layer, False),
        (w["ffn_w_down"], 0, True),
        (w["ln_ffn_g"], layer, False), (w["ln_ffn_b"], layer, False),
    ]
    slabs = FF_CHUNK // LANES
    scratch = [pltpu.VMEM((2, 2, slabs, SUBLANES + SEQ_TILE, LANES), F32),
               pltpu.VMEM((2 * D_FF // LANES, SUBLANES, LANES), F32),
               pltpu.VMEM((2, SEQ_TILE, D_FF), BF16),
               pltpu.VMEM((SEQ_TILE, D_MODEL), F32)]
    batch, seq, d = x.shape
    n_tiles = batch * seq // SEQ_TILE
    c_in, c_out, c_shapes, c_ops = _convert_specs(convert, lambda i: i)
    out, *converted = pl.pallas_call(
        partial(_ffn_kernel, seq // SEQ_TILE, len(convert)),
        name="conv_ffn",
        grid=(n_tiles + 1,),
        in_specs=[pl.BlockSpec((SEQ_TILE, d), lambda i: (jnp.minimum(i, n_tiles - 1), 0))]
        + [_layer_spec(*p) for p in params] + c_in,
        out_specs=[pl.BlockSpec((SEQ_TILE, d), lambda i: (jnp.maximum(i - 1, 0), 0))] + c_out,
        out_shape=[jax.ShapeDtypeStruct((batch * seq, d), x.dtype)] + c_shapes,
        scratch_shapes=scratch,
        compiler_params=pltpu.CompilerParams(
            dimension_semantics=("arbitrary",),
            vmem_limit_bytes=VMEM_LIMIT_BYTES),
    )(x.reshape(batch * seq, d), *[p[0] for p in params], *c_ops)
    return out.reshape(batch, seq, d), converted


def kernel(x, w_in_even, gmlp_ws, gmlp_bs, gmlp_ln_g, gmlp_ln_b, sconv_w, w_out_even, pool_w, pool_scale, ffn_w_up, ffn_b_up, ffn_conv_w, ffn_conv_b, ffn_w_down, ln_mix_g, ln_mix_b, ln_ffn_g, ln_ffn_b):
    assert x.shape[1] % SEQ_TILE == 0 and SEQ_TILE % GMLP_BLOCK == 0
    assert x.shape[1] % POOL_TILE == 0
    assert x.shape[0] * x.shape[1] // SEQ_TILE >= CONVERT_BLOCKS
    w = dict(
        gmlp_ws=gmlp_ws,
        gmlp_bs=jnp.broadcast_to(gmlp_bs[..., None], gmlp_bs.shape + (A_HEAD_DIM,)),
        gmlp_ln_g=_rows(gmlp_ln_g), gmlp_ln_b=_rows(gmlp_ln_b), sconv_w=sconv_w,
        pool_w=pool_w.astype(BF16), pool_scale=_rows(pool_scale),
        ffn_b_up=_rows(ffn_b_up), ffn_conv_w=ffn_conv_w, ffn_conv_b=_rows(ffn_conv_b),
        ln_mix_g=_rows(ln_mix_g), ln_mix_b=_rows(ln_mix_b),
        ln_ffn_g=_rows(ln_ffn_g), ln_ffn_b=_rows(ln_ffn_b),
    )
    ffn_jobs = lambda layer: [(ffn_w_up, layer), (ffn_w_down, layer)]

    def use_ffn(converted):
        w["ffn_w_up"], w["ffn_w_down"] = converted[0][None], converted[1][None]

    w["w_in"], w["w_out"] = w_in_even[:1].astype(BF16), w_out_even[:1].astype(BF16)
    for layer in range(0, DEPTH, 2):
        x, converted = _mixer_even(x, layer // 2, layer, w, ffn_jobs(layer))
        use_ffn(converted)
        x, converted = _ffn(x, layer, w, ffn_jobs(layer + 1))
        use_ffn(converted)
        x = _mixer_odd(x, layer // 2, layer + 1, w)
        more = layer + 2 < DEPTH
        jobs = [(w_in_even, layer // 2 + 1), (w_out_even, layer // 2 + 1)] if more else []
        x, converted = _ffn(x, layer + 1, w, jobs)
        if more:
            w["w_in"], w["w_out"] = converted[0][None], converted[1][None]
    return x
```
